```python
import jax
import jax.numpy as jnp
from jax import lax
import numpy as np

D_MODEL = 1024
BATCH = 8
SEQ = 2048
DEPTH = 1
DEC_BATCH = 128
DEC_SEQ = 4
PAST_LEN = 16384
PAGE_SIZE = 128

HEAD_DIM = 64
MOBA_WIDTH = D_MODEL // 2
MOBA_HEADS = MOBA_WIDTH // HEAD_DIM
MOBA_KV_HEADS = MOBA_HEADS // 2
MOBA_BLOCK = 256
MOBA_TOPK = 3
MOBA_Q_CHUNK = 32
MLA_WIDTH = D_MODEL - MOBA_WIDTH
MLA_V = 64
MLA_HEADS = MLA_WIDTH // MLA_V
MLA_NOPE = 64
MLA_ROPE = 32
MLA_QK = MLA_NOPE + MLA_ROPE
MLA_Q_LORA = 384
MLA_KV_LORA = 128
ROPE_BASE = 10000.0
D_FF = 4 * D_MODEL
ATTN_Q_BLOCK = 128
EPS = 1e-6
MIX_WIDTH = MOBA_WIDTH + MLA_WIDTH
MOBA_KV_WIDTH = MOBA_KV_HEADS * HEAD_DIM
IN_WIDTH = MOBA_WIDTH + 2 * MOBA_KV_WIDTH + MLA_Q_LORA + MLA_KV_LORA + MLA_ROPE
IN_OFFSETS = (MOBA_WIDTH, MOBA_WIDTH + MOBA_KV_WIDTH, MOBA_WIDTH + 2 * MOBA_KV_WIDTH,
              MOBA_WIDTH + 2 * MOBA_KV_WIDTH + MLA_Q_LORA,
              MOBA_WIDTH + 2 * MOBA_KV_WIDTH + MLA_Q_LORA + MLA_KV_LORA)

kernel_name = 'hybrid_moba_mla_decode_step'


def rms_norm(x, g):
    xf = x.astype(jnp.float32)
    y = xf * lax.rsqrt(jnp.mean(xf * xf, axis=-1, keepdims=True) + EPS)
    return y.astype(x.dtype) * g


def rope(x, pos):
    half = x.shape[-1] // 2
    inv_freq = ROPE_BASE ** (-jnp.arange(half, dtype=jnp.float32) / half)
    ang = pos.astype(jnp.float32)[:, None] * inv_freq[None, :]
    shape = (1, pos.shape[0]) + (1,) * (x.ndim - 3) + (half,)
    cos = jnp.cos(ang).reshape(shape)
    sin = jnp.sin(ang).reshape(shape)
    xf = x.astype(jnp.float32)
    x1, x2 = xf[..., :half], xf[..., half:]
    return jnp.concatenate([x1 * cos - x2 * sin, x1 * sin + x2 * cos], axis=-1).astype(x.dtype)


def alibi_slopes(n_heads):
    return jnp.exp2(-8.0 * jnp.arange(1, n_heads + 1, dtype=jnp.float32) / n_heads)


def split_chunks(a, c):
    n = a.shape[1] // c
    return jnp.swapaxes(a.reshape((a.shape[0], n, c) + a.shape[2:]), 0, 1)


def merge_chunks(a):
    a = jnp.swapaxes(a, 0, 1)
    return a.reshape((a.shape[0], a.shape[1] * a.shape[2]) + a.shape[3:])


def moba_attention(q, k_seq, v_seq, q_pos):
    b, t, hq, dh = q.shape
    l, hkv = k_seq.shape[1], k_seq.shape[2]
    nb = -(-l // MOBA_BLOCK)
    pad = nb * MOBA_BLOCK - l

    def to_blocks(a):
        a = jnp.pad(a, ((0, 0), (0, pad), (0, 0), (0, 0)))
        return a.reshape(b, nb, MOBA_BLOCK, hkv, dh).transpose(0, 3, 1, 2, 4)

    kb = to_blocks(k_seq)
    vb = to_blocks(v_seq)
    kv_of_head = jnp.arange(hq) // (hq // hkv)
    kmean = jnp.mean(kb.astype(jnp.float32), axis=3)[:, kv_of_head]
    slopes = alibi_slopes(hq)[:, None, None, None]
    n_sel = min(MOBA_TOPK, nb)
    b_ix = jnp.arange(b)[:, None, None, None]
    h_ix = kv_of_head[None, :, None, None]
    offs = jnp.arange(MOBA_BLOCK, dtype=jnp.int32)
    scale = dh ** -0.5

    def attend(args):
        q_c, pos_c = args
        c = q_c.shape[1]
        own = pos_c // MOBA_BLOCK
        gate = jnp.einsum('bchd,bhnd->bhcn', q_c.astype(jnp.float32), kmean)
        is_past = jnp.arange(nb, dtype=jnp.int32)[None, :] < own[:, None]
        _, sel = lax.top_k(jnp.where(is_past, gate, -jnp.inf), n_sel)
        own_b = jnp.broadcast_to(own[:, None], (b, hq, c, 1)).astype(sel.dtype)
        idx = jnp.concatenate([sel, own_b], axis=-1)
        ok = jnp.concatenate([sel < own[:, None], jnp.ones((b, hq, c, 1), dtype=bool)], axis=-1)
        kg = kb[b_ix, h_ix, idx]
        vg = vb[b_ix, h_ix, idx]
        dist = (pos_c[:, None, None] - (idx[..., None] * MOBA_BLOCK + offs)).astype(jnp.float32)
        s = jnp.einsum('bchd,bhcnsd->bhcns', q_c, kg).astype(jnp.float32) * scale - slopes * dist
        s = jnp.where(ok[..., None] & (dist >= 0), s, -jnp.inf).reshape(b, hq, c, -1)
        p = jax.nn.softmax(s, axis=-1).astype(vg.dtype)
        return jnp.einsum('bhcm,bhcmd->bchd', p, vg.reshape(b, hq, c, -1, dh))

    c = MOBA_Q_CHUNK if t % MOBA_Q_CHUNK == 0 else t
    out = lax.map(attend, (split_chunks(q, c), q_pos.reshape(-1, c)))
    return merge_chunks(out)


def mla_attention(q_nope, q_rope, c_seq, kr_seq, q_pos, w_uk, w_uv, k_g):
    l = c_seq.shape[1]
    key_pos = jnp.arange(l, dtype=jnp.int32)
    k_nope = jnp.einsum('blr,rhd->blhd', c_seq, w_uk)
    ms = (jnp.sum(jnp.square(k_nope.astype(jnp.float32)), axis=-1)
          + jnp.sum(jnp.square(kr_seq.astype(jnp.float32)), axis=-1)[:, :, None]) / MLA_QK
    inv_t = jnp.swapaxes(lax.rsqrt(ms + EPS), 1, 2)[:, :, None, :] * (MLA_QK ** -0.5)
    kr_rot = rope(kr_seq * k_g[MLA_NOPE:], key_pos)
    qn = q_nope * k_g[:MLA_NOPE]

    def attend(args):
        qn_c, qr_c, pos_c = args
        s = jnp.einsum('bthd,blhd->bhtl', qn_c, k_nope) + jnp.einsum('bthd,bld->bhtl', qr_c, kr_rot)
        s = s.astype(jnp.float32) * inv_t
        s = jnp.where(key_pos[None, None, None, :] <= pos_c[None, None, :, None], s, -jnp.inf)
        p = jax.nn.softmax(s, axis=-1).astype(c_seq.dtype)
        o_lat = jnp.einsum('bhtl,blr->bthr', p, c_seq)
        return jnp.einsum('bthr,rhd->bthd', o_lat, w_uv)

    t = q_nope.shape[1]
    c = ATTN_Q_BLOCK if t % ATTN_Q_BLOCK == 0 else t
    out = lax.map(attend, (split_chunks(qn, c), split_chunks(q_rope, c), q_pos.reshape(-1, c)))
    return merge_chunks(out)


def decoder_layer(x, pos, k_past, v_past, ckv_past, kr_past,
                  ln1_g, w_in, moba_q_g, moba_k_g, mla_q_lora_g, w_uq, mla_q_g,
                  mla_kv_lora_g, w_uk, w_uv, mla_k_g, out_g_moba, out_g_mla, w_o,
                  ln2_g, w_up, w_down):
    b, t, _ = x.shape
    h = rms_norm(x, ln1_g)
    z = jnp.einsum('btd,de->bte', h, w_in)
    q_a, k_a, v_a, cq, ckv, kr = jnp.split(z, IN_OFFSETS, axis=-1)
    q_m = rms_norm(q_a.reshape(b, t, MOBA_HEADS, HEAD_DIM), moba_q_g)
    k_m = rms_norm(k_a.reshape(b, t, MOBA_KV_HEADS, HEAD_DIM), moba_k_g)
    v_m = v_a.reshape(b, t, MOBA_KV_HEADS, HEAD_DIM)
    o_moba = moba_attention(q_m, jnp.concatenate([k_past, k_m], axis=1),
                            jnp.concatenate([v_past, v_m], axis=1), pos)
    q_h = jnp.einsum('btr,re->bte', rms_norm(cq, mla_q_lora_g), w_uq)
    q_h = rms_norm(q_h.reshape(b, t, MLA_HEADS, MLA_QK), mla_q_g)
    q_nope = q_h[..., :MLA_NOPE]
    q_rope = rope(q_h[..., MLA_NOPE:], pos)
    ckv_n = rms_norm(ckv, mla_kv_lora_g)
    o_mla = mla_attention(q_nope, q_rope, jnp.concatenate([ckv_past, ckv_n], axis=1),
                          jnp.concatenate([kr_past, kr], axis=1), pos, w_uk, w_uv, mla_k_g)
    mixed = jnp.concatenate([rms_norm(o_moba.reshape(b, t, MOBA_WIDTH), out_g_moba),
                             rms_norm(o_mla.reshape(b, t, MLA_WIDTH), out_g_mla)], axis=-1)
    x = x + jnp.einsum('bte,ed->btd', mixed, w_o)
    u = jax.nn.relu(jnp.einsum('btd,df->btf', rms_norm(x, ln2_g), w_up))
    x = x + jnp.einsum('btf,fd->btd', u * u, w_down)
    return x, k_m, v_m, ckv_n, kr


def setup_inputs(seed: int = 0) -> dict:
    key = jax.random.key(seed)
    ks = jax.random.split(key, 28)
    n_pages = PAST_LEN // PAGE_SIZE
    n_pool = (DEC_BATCH * n_pages * 5) // 4

    def nrm(k, shape, s):
        return jax.random.normal(k, shape, jnp.float32) * s

    def gain(k, n):
        return 1.0 + 0.02 * jax.random.normal(k, (DEPTH, n), jnp.float32)

    page_table = jax.random.permutation(ks[6], n_pool)[:DEC_BATCH * n_pages]
    page_table = page_table.reshape(DEC_BATCH, n_pages).astype(jnp.int32)
    return {
        'x_prompt': nrm(ks[0], (BATCH, SEQ, D_MODEL), 1.0),
        'x_sample': nrm(ks[1], (DEC_BATCH, DEC_SEQ, D_MODEL), 1.0),
        'cache_moba_k': nrm(ks[2], (DEPTH, n_pool, PAGE_SIZE, MOBA_KV_HEADS, HEAD_DIM), 1.0),
        'cache_moba_v': nrm(ks[3], (DEPTH, n_pool, PAGE_SIZE, MOBA_KV_HEADS, HEAD_DIM), 1.0),
        'cache_mla_ckv': nrm(ks[4], (DEPTH, n_pool, PAGE_SIZE, MLA_KV_LORA), 1.0),
        'cache_mla_krope': nrm(ks[5], (DEPTH, n_pool, PAGE_SIZE, MLA_ROPE), 1.0),
        'page_table': page_table,
        'ln1_g': gain(ks[7], D_MODEL),
        'w_in': nrm(ks[8], (DEPTH, D_MODEL, IN_WIDTH), D_MODEL ** -0.5),
        'moba_q_g': gain(ks[9], HEAD_DIM),
        'moba_k_g': gain(ks[10], HEAD_DIM),
        'mla_q_lora_g': gain(ks[11], MLA_Q_LORA),
        'w_uq': nrm(ks[12], (DEPTH, MLA_Q_LORA, MLA_HEADS * MLA_QK), MLA_Q_LORA ** -0.5),
        'mla_q_g': gain(ks[13], MLA_QK),
        'mla_kv_lora_g': gain(ks[14], MLA_KV_LORA),
        'w_uk': nrm(ks[15], (DEPTH, MLA_KV_LORA, MLA_HEADS, MLA_NOPE), MLA_KV_LORA ** -0.5),
        'w_uv': nrm(ks[16], (DEPTH, MLA_KV_LORA, MLA_HEADS, MLA_V), MLA_KV_LORA ** -0.5),
        'mla_k_g': gain(ks[17], MLA_QK),
        'out_g_moba': gain(ks[18], MOBA_WIDTH),
        'out_g_mla': gain(ks[19], MLA_WIDTH),
        'w_o': nrm(ks[20], (DEPTH, MIX_WIDTH, D_MODEL), MIX_WIDTH ** -0.5),
        'ln2_g': gain(ks[21], D_MODEL),
        'w_up': nrm(ks[22], (DEPTH, D_MODEL, D_FF), D_MODEL ** -0.5),
        'w_down': nrm(ks[23], (DEPTH, D_FF, D_MODEL), D_FF ** -0.5),
    }


def reference(x_prompt, x_sample, cache_moba_k, cache_moba_v, cache_mla_ckv, cache_mla_krope,
              page_table, ln1_g, w_in, moba_q_g, moba_k_g, mla_q_lora_g, w_uq, mla_q_g,
              mla_kv_lora_g, w_uk, w_uv, mla_k_g, out_g_moba, out_g_mla, w_o, ln2_g, w_up, w_down):
    n_seq, n_pages = page_table.shape
    page = cache_moba_k.shape[2]
    past_len = n_pages * page
    b_p, t_p, _ = x_prompt.shape
    t_s = x_sample.shape[1]
    dt = x_prompt.dtype
    pos_p = jnp.arange(t_p, dtype=jnp.int32)
    pos_s = past_len + jnp.arange(t_s, dtype=jnp.int32)

    def gather(pool, layer):
        rows = pool[layer, page_table]
        return rows.reshape((n_seq, past_len) + pool.shape[3:])

    y_prompt, y_sample = x_prompt, x_sample
    kp_l, vp_l, cp_l, rp_l, ks_l, vs_l, cs_l, rs_l = [], [], [], [], [], [], [], []
    for layer in range(DEPTH):
        w = (ln1_g[layer], w_in[layer], moba_q_g[layer], moba_k_g[layer], mla_q_lora_g[layer],
             w_uq[layer], mla_q_g[layer], mla_kv_lora_g[layer], w_uk[layer], w_uv[layer],
             mla_k_g[layer], out_g_moba[layer], out_g_mla[layer], w_o[layer], ln2_g[layer],
             w_up[layer], w_down[layer])
        y_prompt, kp, vp, cp, rp = decoder_layer(
            y_prompt, pos_p,
            jnp.zeros((b_p, 0, MOBA_KV_HEADS, HEAD_DIM), dt), jnp.zeros((b_p, 0, MOBA_KV_HEADS, HEAD_DIM), dt),
            jnp.zeros((b_p, 0, MLA_KV_LORA), dt), jnp.zeros((b_p, 0, MLA_ROPE), dt), *w)
        y_sample, ks, vs, cs, rs = decoder_layer(
            y_sample, pos_s, gather(cache_moba_k, layer), gather(cache_moba_v, layer),
            gather(cache_mla_ckv, layer), gather(cache_mla_krope, layer), *w)
        kp_l.append(kp); vp_l.append(vp); cp_l.append(cp); rp_l.append(rp)
        ks_l.append(ks); vs_l.append(vs); cs_l.append(cs); rs_l.append(rs)
    new_moba_k_prompt = jnp.stack(kp_l)
    new_moba_v_prompt = jnp.stack(vp_l)
    new_mla_ckv_prompt = jnp.stack(cp_l)
    new_mla_krope_prompt = jnp.stack(rp_l)
    new_moba_k_sample = jnp.stack(ks_l)
    new_moba_v_sample = jnp.stack(vs_l)
    new_mla_ckv_sample = jnp.stack(cs_l)
    new_mla_krope_sample = jnp.stack(rs_l)
    return (y_prompt, y_sample, new_moba_k_prompt, new_moba_v_prompt, new_mla_ckv_prompt,
            new_mla_krope_prompt, new_moba_k_sample, new_moba_v_sample, new_mla_ckv_sample,
            new_mla_krope_sample)
```

```python
import functools

import jax
import jax.numpy as jnp
from jax import lax
from jax.experimental import pallas as pl
from jax.experimental.pallas import tpu as pltpu

F32 = jnp.float32
BF16 = jnp.bfloat16

D_MODEL = 1024
HEAD_DIM = 64
MOBA_HEADS = 8
MOBA_KV_HEADS = 4
MOBA_WIDTH = MOBA_HEADS * HEAD_DIM
MOBA_KV_WIDTH = MOBA_KV_HEADS * HEAD_DIM
MOBA_BLOCK = 256
MOBA_TOPK = 3
MLA_HEADS = 8
MLA_V = 64
MLA_WIDTH = MLA_HEADS * MLA_V
MLA_NOPE = 64
MLA_ROPE = 32
MLA_HALF = MLA_ROPE // 2
MLA_QK = MLA_NOPE + MLA_ROPE
MLA_Q_LORA = 384
MLA_KV_LORA = 128
ROPE_BASE = 10000.0
D_FF = 4 * D_MODEL
EPS = 1e-6
MAIN_WIDTH = MOBA_WIDTH + 2 * MOBA_KV_WIDTH + MLA_Q_LORA + MLA_KV_LORA

LANES = 128
SUBLANES = 8
VMEM_LIMIT_BYTES = 56 * 1024 * 1024

ROW_TILE = 512
Q_TILE = MOBA_BLOCK
KEY_TILE = 256
NEG = -1e30

_NT = (((1,), (1,)), ((), ()))


def _dot(a, b):
    return jnp.dot(a, b, preferred_element_type=F32)


def _dot_nt(a, b):
    return lax.dot_general(a, b, _NT, preferred_element_type=F32)


def _split_bf16(a):
    hi = a.astype(BF16)
    lo = (a - hi.astype(F32)).astype(BF16)
    return hi, lo


def _dot_nt_f32(a, b):
    ah, al = _split_bf16(a)
    bh, bl = _split_bf16(b)
    return _dot_nt(ah, bh) + _dot_nt(ah, bl) + _dot_nt(al, bh)


def _rms(x, g):
    return x * lax.rsqrt(jnp.mean(x * x, axis=-1, keepdims=True) + EPS) * g


def _const_spec(shape):
    nd = len(shape)
    return pl.BlockSpec(shape, lambda *_: (0,) * nd)


def _inproj_kernel(x_ref, ln1_ref, wmain_ref, wkr_ref, gq_ref, gk_ref, gql_ref, wuq_ref, gmq_ref,
                   kgn_ref, gkv_ref, rc_ref, rs1_ref, rs2_ref,
                   qm_ref, km_ref, vm_ref, qmla_ref, ckv_ref, kr_ref):
    x = x_ref[...]
    hb = _rms(x, ln1_ref[...]).astype(BF16)
    z = _dot(hb, wmain_ref[...])
    zk = _dot(hb, wkr_ref[...])
    tm = x.shape[0]
    lane = lax.broadcasted_iota(jnp.int32, (tm, LANES), 1)
    low = lane < HEAD_DIM

    def pair_norm(blk, g):
        sq = blk * blk
        lo = jnp.sum(jnp.where(low, sq, 0.0), axis=-1, keepdims=True)
        hi = jnp.sum(jnp.where(low, 0.0, sq), axis=-1, keepdims=True)
        inv = jnp.where(low, lax.rsqrt(lo / HEAD_DIM + EPS), lax.rsqrt(hi / HEAD_DIM + EPS))
        return blk * inv * g

    for j in range(MOBA_WIDTH // LANES):
        qm_ref[:, j * LANES:(j + 1) * LANES] = pair_norm(z[:, j * LANES:(j + 1) * LANES], gq_ref[...])
    for j in range(MOBA_KV_WIDTH // LANES):
        o = MOBA_WIDTH + j * LANES
        km_ref[:, j * LANES:(j + 1) * LANES] = pair_norm(z[:, o:o + LANES], gk_ref[...])
    o = MOBA_WIDTH + MOBA_KV_WIDTH
    vm_ref[...] = z[:, o:o + MOBA_KV_WIDTH]
    o += MOBA_KV_WIDTH
    cqn = _rms(z[:, o:o + MLA_Q_LORA], gql_ref[...]).astype(BF16)
    o += MLA_Q_LORA
    ckv_ref[...] = _rms(z[:, o:o + MLA_KV_LORA], gkv_ref[...])
    kr_ref[...] = zk[:, :MLA_ROPE]

    qh = _dot(cqn, wuq_ref[...])
    rc, rs1, rs2 = rc_ref[...], rs1_ref[...], rs2_ref[...]
    gain = gmq_ref[...]
    kgn = kgn_ref[...]
    for h in range(MLA_HEADS):
        blk = qh[:, h * LANES:(h + 1) * LANES]
        ss = jnp.sum(blk * blk, axis=-1, keepdims=True)
        y = blk * lax.rsqrt(ss / MLA_QK + EPS) * gain
        y = y * rc + pltpu.roll(y, MLA_HALF, axis=1) * rs1 + pltpu.roll(y, LANES - MLA_HALF, axis=1) * rs2
        qmla_ref[:, h * LANES:(h + 1) * LANES] = y * kgn


def _inproj(x2d, tabs, n_tab, w):
    rows = x2d.shape[0]
    tm = min(ROW_TILE, rows)
    assert rows % tm == 0 and tm % SUBLANES == 0
    row_spec = lambda width: pl.BlockSpec((tm, width), lambda i: (i, 0))
    tab_spec = pl.BlockSpec((tm, LANES), lambda i: (i % n_tab, 0))
    consts = [w['ln1'], w['w_main'], w['w_kr'], w['gq'], w['gk'], w['gql'], w['w_uq'], w['gmq'], w['kgn'], w['gkv']]
    out_widths = (MOBA_WIDTH, MOBA_KV_WIDTH, MOBA_KV_WIDTH, MLA_HEADS * LANES, MLA_KV_LORA, MLA_ROPE)
    return pl.pallas_call(
        _inproj_kernel,
        grid=(rows // tm,),
        in_specs=[row_spec(D_MODEL)] + [_const_spec(c.shape) for c in consts] + [tab_spec] * 3,
        out_specs=[row_spec(wd) for wd in out_widths],
        out_shape=[jax.ShapeDtypeStruct((rows, wd), F32) for wd in out_widths],
        compiler_params=pltpu.CompilerParams(dimension_semantics=("parallel",), vmem_limit_bytes=VMEM_LIMIT_BYTES),
        name="inproj",
    )(x2d, *consts, *tabs)


def _moba_prompt_kernel(q_ref, k_ref, v_ref, slope_ref, o_ref, kmean_ref, *, nb):
    i = pl.program_id(1)
    nbp = kmean_ref.shape[0]

    @pl.when(i == 0)
    def _():
        kmean_ref[...] = jnp.zeros_like(kmean_ref)
        for n in range(nb):
            kmean_ref[n:n + 1, :] = jnp.mean(k_ref[0, n * MOBA_BLOCK:(n + 1) * MOBA_BLOCK, :], axis=0, keepdims=True)

    scale = HEAD_DIM ** -0.5
    col = lax.broadcasted_iota(jnp.int32, (Q_TILE, nbp), 1)
    past = col < i
    qrow = lax.broadcasted_iota(jnp.int32, (Q_TILE, KEY_TILE), 0)
    kcol = lax.broadcasted_iota(jnp.int32, (Q_TILE, KEY_TILE), 1)
    rel = (qrow - kcol).astype(F32)

    for h in range(MOBA_HEADS):
        g = h // (MOBA_HEADS // MOBA_KV_HEADS)
        qh = q_ref[0, :, h * HEAD_DIM:(h + 1) * HEAD_DIM]
        gate = _dot_nt_f32(qh, kmean_ref[:, g * HEAD_DIM:(g + 1) * HEAD_DIM])
        sel = jnp.zeros((Q_TILE, nbp), F32)
        for n in range(nb):
            gn = gate[:, n:n + 1]
            beats = ((gate > gn) | ((gate == gn) & (col < n))) & past
            rank = jnp.sum(beats.astype(F32), axis=-1, keepdims=True)
            sel = jnp.where((col == n) & (rank < MOBA_TOPK) & past, 1.0, sel)
        sel = jnp.where(col == i, 1.0, sel)
        qb = qh.astype(BF16)
        slope = slope_ref[h]

        def body(n, carry):
            m, l, acc = carry
            start = pl.multiple_of(n * MOBA_BLOCK, MOBA_BLOCK)
            kb = k_ref[0, pl.ds(start, MOBA_BLOCK), g * HEAD_DIM:(g + 1) * HEAD_DIM].astype(BF16)
            vb = v_ref[0, pl.ds(start, MOBA_BLOCK), g * HEAD_DIM:(g + 1) * HEAD_DIM].astype(BF16)
            dist = rel + ((i - n) * MOBA_BLOCK).astype(F32)
            s = _dot_nt(qb, kb) * scale - slope * dist
            seln = jnp.sum(jnp.where(col == n, sel, 0.0), axis=-1, keepdims=True)
            mask = (seln > 0.5) & (dist >= 0.0)
            s = jnp.where(mask, s, NEG)
            m_new = jnp.maximum(m, jnp.max(s, axis=-1, keepdims=True))
            alpha = jnp.exp(m - m_new)
            p = jnp.where(mask, jnp.exp(s - m_new), 0.0)
            l = alpha * l + jnp.sum(p, axis=-1, keepdims=True)
            acc = alpha * acc + _dot(p.astype(BF16), vb)
            return m_new, l, acc

        init = (jnp.full((Q_TILE, 1), NEG, F32), jnp.zeros((Q_TILE, 1), F32), jnp.zeros((Q_TILE, HEAD_DIM), F32))
        _, l, acc = lax.fori_loop(0, i + 1, body, init)
        o_ref[0, :, h * HEAD_DIM:(h + 1) * HEAD_DIM] = acc / l


def _moba_prompt(qm, km, vm, slopes):
    b, t, _ = qm.shape
    assert t % MOBA_BLOCK == 0
    nb = t // MOBA_BLOCK
    assert nb >= MOBA_TOPK
    nbp = -(-nb // SUBLANES) * SUBLANES
    return pl.pallas_call(
        functools.partial(_moba_prompt_kernel, nb=nb),
        grid=(b, nb),
        in_specs=[pl.BlockSpec((1, Q_TILE, MOBA_WIDTH), lambda bi, i: (bi, i, 0)),
                  pl.BlockSpec((1, t, MOBA_KV_WIDTH), lambda bi, i: (bi, 0, 0)),
                  pl.BlockSpec((1, t, MOBA_KV_WIDTH), lambda bi, i: (bi, 0, 0)),
                  pl.BlockSpec(memory_space=pltpu.SMEM)],
        out_specs=pl.BlockSpec((1, Q_TILE, MOBA_WIDTH), lambda bi, i: (bi, i, 0)),
        out_shape=jax.ShapeDtypeStruct((b, t, MOBA_WIDTH), F32),
        scratch_shapes=[pltpu.VMEM((nbp, MOBA_KV_WIDTH), F32)],
        compiler_params=pltpu.CompilerParams(dimension_semantics=("parallel", "arbitrary"),
                                             vmem_limit_bytes=VMEM_LIMIT_BYTES),
        name="moba_prompt",
    )(qm, km, vm, slopes)


def _mla_key_tile(c_bf, kr, wukT, kgr, cos, sin, eye):
    kT = _dot_nt(wukT, c_bf)
    n_keys = kT.shape[1]
    ssq = jnp.sum((kT * kT).reshape(MLA_HEADS, MLA_NOPE, n_keys), axis=1)
    krT = _dot_nt_f32(eye, kr)
    krsq = jnp.sum(krT * krT, axis=0, keepdims=True)
    inv = lax.rsqrt((ssq + krsq) / MLA_QK + EPS) * (MLA_QK ** -0.5)
    x = krT * kgr
    x1, x2 = x[:MLA_HALF], x[MLA_HALF:]
    rot = jnp.concatenate([x1 * cos - x2 * sin, x1 * sin + x2 * cos], axis=0)
    return kT, rot, inv


def _mla_prompt_kernel(q_ref, c_ref, kr_ref, wukT_ref, wuv_ref, kgr_ref, cos_ref, sin_ref, eye_ref,
                       o_ref, khat_ref, vhat_ref, *, nb):
    i = pl.program_id(1)

    @pl.when(i == 0)
    def _():
        for n in range(nb):
            rows = slice(n * KEY_TILE, (n + 1) * KEY_TILE)
            cb = c_ref[0, rows, :].astype(BF16)
            kT, rot, inv = _mla_key_tile(cb, kr_ref[0, rows, :], wukT_ref[...], kgr_ref[...],
                                         cos_ref[:, rows], sin_ref[:, rows], eye_ref[...])
            pad = jnp.zeros((LANES - MLA_QK, KEY_TILE), F32)
            for h in range(MLA_HEADS):
                sc = inv[h:h + 1]
                kh = jnp.concatenate([kT[h * MLA_NOPE:(h + 1) * MLA_NOPE] * sc, rot * sc, pad], axis=0)
                khat_ref[h, n] = kh.astype(BF16)
            vhat_ref[n] = _dot(cb, wuv_ref[...]).astype(BF16)

    qrow = lax.broadcasted_iota(jnp.int32, (Q_TILE, KEY_TILE), 0)
    kcol = lax.broadcasted_iota(jnp.int32, (Q_TILE, KEY_TILE), 1)

    for h in range(MLA_HEADS):
        qb = q_ref[0, :, h * LANES:(h + 1) * LANES].astype(BF16)

        def body(n, carry):
            m, l, acc = carry
            s = _dot(qb, khat_ref[h, n])
            mask = (kcol + n * KEY_TILE) <= (qrow + i * KEY_TILE)
            s = jnp.where(mask, s, NEG)
            m_new = jnp.maximum(m, jnp.max(s, axis=-1, keepdims=True))
            alpha = jnp.exp(m - m_new)
            p = jnp.where(mask, jnp.exp(s - m_new), 0.0)
            l = alpha * l + jnp.sum(p, axis=-1, keepdims=True)
            acc = alpha * acc + _dot(p.astype(BF16), vhat_ref[n, :, h * MLA_V:(h + 1) * MLA_V])
            return m_new, l, acc

        init = (jnp.full((Q_TILE, 1), NEG, F32), jnp.zeros((Q_TILE, 1), F32), jnp.zeros((Q_TILE, MLA_V), F32))
        _, l, acc = lax.fori_loop(0, i + 1, body, init)
        o_ref[0, :, h * MLA_V:(h + 1) * MLA_V] = acc / l


def _mla_prompt(qmla, ckv, kr, w, cos_t, sin_t):
    b, t, _ = qmla.shape
    assert t % KEY_TILE == 0
    nb = t // KEY_TILE
    consts = [w['w_ukT'], w['w_uv'], w['kgr'], cos_t, sin_t, w['eye']]
    return pl.pallas_call(
        functools.partial(_mla_prompt_kernel, nb=nb),
        grid=(b, nb),
        in_specs=[pl.BlockSpec((1, Q_TILE, MLA_HEADS * LANES), lambda bi, i: (bi, i, 0)),
                  pl.BlockSpec((1, t, MLA_KV_LORA), lambda bi, i: (bi, 0, 0)),
                  pl.BlockSpec((1, t, MLA_ROPE), lambda bi, i: (bi, 0, 0))]
                 + [_const_spec(c.shape) for c in consts],
        out_specs=pl.BlockSpec((1, Q_TILE, MLA_WIDTH), lambda bi, i: (bi, i, 0)),
        out_shape=jax.ShapeDtypeStruct((b, t, MLA_WIDTH), F32),
        scratch_shapes=[pltpu.VMEM((MLA_HEADS, nb, LANES, KEY_TILE), BF16),
                        pltpu.VMEM((nb, KEY_TILE, MLA_WIDTH), BF16)],
        compiler_params=pltpu.CompilerParams(dimension_semantics=("parallel", "arbitrary"),
                                             vmem_limit_bytes=VMEM_LIMIT_BYTES),
        name="mla_prompt",
    )(qmla, ckv, kr, *consts)


def _moba_decode_kernel(pt_ref, qbd_ref, knew_ref, vnew_ref, slope_ref, tq_ref, *rest, pps, past_len):
    del pt_ref
    kpages, vpages = rest[:pps], rest[pps:2 * pps]
    o_ref = rest[2 * pps]
    gate_s, m_s, l_s, o_s = rest[2 * pps + 1:]
    j = pl.program_id(1)
    nblk = gate_s.shape[0]
    nq = qbd_ref.shape[1]
    scale = HEAD_DIM ** -0.5
    bps = pps // 2

    qbd = qbd_ref[0]
    qb = qbd.astype(BF16)
    slope = slope_ref[...]
    tq = tq_ref[...]
    slope2 = jnp.concatenate([slope, slope], axis=1)
    tq2 = jnp.concatenate([tq, tq], axis=1)
    lane2 = lax.broadcasted_iota(jnp.int32, (nq, MOBA_BLOCK), 1).astype(F32)

    for bb in range(bps):
        k0, k1 = kpages[2 * bb][0], kpages[2 * bb + 1][0]
        kmean = (jnp.sum(k0, axis=0, keepdims=True) + jnp.sum(k1, axis=0, keepdims=True)) / MOBA_BLOCK
        gate = jnp.sum(qbd * kmean, axis=-1, keepdims=True)
        kb = jnp.concatenate([k0, k1], axis=0).astype(BF16)
        vb = jnp.concatenate([vpages[2 * bb][0], vpages[2 * bb + 1][0]], axis=0).astype(BF16)
        blk = j * bps + bb
        dist = (past_len + tq2) - ((blk * MOBA_BLOCK).astype(F32) + lane2)
        s = _dot_nt(qb, kb) * scale - slope2 * dist
        m = jnp.max(s, axis=-1, keepdims=True)
        p = jnp.exp(s - m)
        l = jnp.sum(p, axis=-1, keepdims=True)
        gate_s[blk] = jnp.broadcast_to(gate, (nq, LANES))
        m_s[blk] = jnp.broadcast_to(m, (nq, LANES))
        l_s[blk] = jnp.broadcast_to(l, (nq, LANES))
        o_s[blk] = _dot(p.astype(BF16), vb)

    @pl.when(j == pl.num_programs(1) - 1)
    def _():
        gates = gate_s[...]
        bidx = lax.broadcasted_iota(jnp.int32, gates.shape, 0)
        picked = jnp.zeros(gates.shape, F32)
        for _ in range(min(MOBA_TOPK, nblk)):
            mx = jnp.max(gates, axis=0, keepdims=True)
            first = jnp.min(jnp.where(gates == mx, bidx, nblk), axis=0, keepdims=True)
            hit = bidx == first
            picked = jnp.where(hit, 1.0, picked)
            gates = jnp.where(hit, -jnp.inf, gates)
        sel = picked > 0.5

        n_new = knew_ref.shape[1]
        tk = lax.broadcasted_iota(jnp.int32, (nq, n_new), 1).astype(F32)
        tq_n = tq[:, :n_new]
        dist_o = tq_n - tk
        s_o = _dot_nt(qb, knew_ref[0].astype(BF16)) * scale - slope[:, :n_new] * dist_o
        mask_o = dist_o >= 0.0
        s_o = jnp.where(mask_o, s_o, NEG)
        m_o = jnp.broadcast_to(jnp.max(s_o, axis=-1, keepdims=True), (nq, LANES))

        m_all = m_s[...]
        m_fin = jnp.maximum(jnp.max(jnp.where(sel, m_all, NEG), axis=0), m_o)
        wgt = jnp.exp(jnp.where(sel, m_all - m_fin[None], -jnp.inf))
        p_o = jnp.where(mask_o, jnp.exp(s_o - m_fin[:, :n_new]), 0.0)
        l_fin = jnp.sum(wgt * l_s[...], axis=0) + jnp.sum(p_o, axis=-1, keepdims=True)
        wgt2 = jnp.concatenate([wgt, wgt], axis=2)
        o_fin = jnp.sum(wgt2 * o_s[...], axis=0) + _dot(p_o.astype(BF16), vnew_ref[0].astype(BF16))
        o_fin = o_fin / jnp.concatenate([l_fin, l_fin], axis=1)

        row = lax.broadcasted_iota(jnp.int32, (nq, HEAD_DIM), 0)
        grp = (row % MOBA_HEADS) // (MOBA_HEADS // MOBA_KV_HEADS)
        out = jnp.zeros((nq, HEAD_DIM), F32)
        for g in range(MOBA_KV_HEADS):
            out = jnp.where(grp == g, o_fin[:, g * HEAD_DIM:(g + 1) * HEAD_DIM], out)
        o_ref[0] = out


def _pages_per_step(n_pages):
    pps = min(16, n_pages)
    assert n_pages % pps == 0 and pps % 2 == 0
    return pps


def _moba_decode(page_flat, qbd, knew, vnew, slope_rows, tq_rows, pool_k, pool_v, n_pages, page):
    s, nq, _ = qbd.shape
    assert 2 * page == MOBA_BLOCK
    pps = _pages_per_step(n_pages)
    nblk = n_pages // 2
    n_new = knew.shape[1]

    def page_spec(kk):
        return pl.BlockSpec((1, page, MOBA_KV_WIDTH), lambda si, j, pt: (pt[si * n_pages + j * pps + kk], 0, 0))

    per_seq = lambda shape: pl.BlockSpec((1,) + shape, lambda si, j, pt: (si, 0, 0))
    const2 = pl.BlockSpec((nq, LANES), lambda si, j, pt: (0, 0))
    grid_spec = pltpu.PrefetchScalarGridSpec(
        num_scalar_prefetch=1,
        grid=(s, n_pages // pps),
        in_specs=[per_seq((nq, MOBA_KV_WIDTH)), per_seq((n_new, MOBA_KV_WIDTH)), per_seq((n_new, MOBA_KV_WIDTH)),
                  const2, const2]
                 + [page_spec(kk) for kk in range(pps)] * 2,
        out_specs=per_seq((nq, HEAD_DIM)),
        scratch_shapes=[pltpu.VMEM((nblk, nq, LANES), F32)] * 3 + [pltpu.VMEM((nblk, nq, MOBA_KV_WIDTH), F32)],
    )
    return pl.pallas_call(
        functools.partial(_moba_decode_kernel, pps=pps, past_len=float(n_pages * page)),
        grid_spec=grid_spec,
        out_shape=jax.ShapeDtypeStruct((s, nq, HEAD_DIM), F32),
        compiler_params=pltpu.CompilerParams(dimension_semantics=("parallel", "arbitrary"),
                                             vmem_limit_bytes=VMEM_LIMIT_BYTES),
        name="moba_decode",
    )(page_flat, qbd, knew, vnew, slope_rows, tq_rows, *([pool_k] * pps), *([pool_v] * pps))


def _mla_decode_kernel(pt_ref, qbdn_ref, qr_ref, cnew_ref, krnew_ref, tq_ref, wukT_ref, wuv_ref, kgr_ref,
                       cos_ref, sin_ref, cosn_ref, sinn_ref, eye_ref, *rest, pps, n_tok):
    del pt_ref
    cpages, rpages = rest[:pps], rest[pps:2 * pps]
    o_ref = rest[2 * pps]
    qabs_s, m_s, l_s, acc_s = rest[2 * pps + 1:]
    j = pl.program_id(1)
    nq = qr_ref.shape[1]
    page = cpages[0].shape[1]

    @pl.when(j == 0)
    def _():
        qabs_s[...] = _dot(qbdn_ref[0].astype(BF16), wukT_ref[...])
        m_s[...] = jnp.full(m_s.shape, NEG, F32)
        l_s[...] = jnp.zeros(l_s.shape, F32)
        acc_s[...] = jnp.zeros(acc_s.shape, F32)

    qabs = qabs_s[...].astype(BF16)
    qr = qr_ref[0].astype(BF16)

    def update(c, kr, cos, sin, mask):
        cb = c.astype(BF16)
        n_keys = c.shape[0]
        kT, rot, inv = _mla_key_tile(cb, kr, wukT_ref[...], kgr_ref[...], cos, sin, eye_ref[...])
        del kT
        inv_rows = jnp.concatenate([inv] * (nq // MLA_HEADS), axis=0)
        s = (_dot_nt(qabs, cb) + _dot(qr, rot.astype(BF16))) * inv_rows
        if mask is not None:
            s = jnp.where(mask, s, NEG)
        m_old = m_s[...]
        m_new = jnp.maximum(m_old, jnp.max(s, axis=-1, keepdims=True))
        alpha = jnp.exp(m_old - m_new)
        reps = n_keys // LANES
        m_b = jnp.concatenate([m_new] * reps, axis=1) if reps > 1 else m_new
        p = jnp.exp(s - m_b)
        if mask is not None:
            p = jnp.where(mask, p, 0.0)
        l_s[...] = alpha * l_s[...] + jnp.sum(p, axis=-1, keepdims=True)
        acc_s[...] = alpha * acc_s[...] + _dot(p.astype(BF16), cb)
        m_s[...] = m_new

    for bb in range(pps // 2):
        c = jnp.concatenate([cpages[2 * bb][0], cpages[2 * bb + 1][0]], axis=0)
        kr = jnp.concatenate([rpages[2 * bb][0], rpages[2 * bb + 1][0]], axis=0)
        cols = slice(bb * 2 * page, (bb + 1) * 2 * page)
        update(c, kr, cos_ref[:, cols], sin_ref[:, cols], None)

    @pl.when(j == pl.num_programs(1) - 1)
    def _():
        tk = lax.broadcasted_iota(jnp.int32, (nq, LANES), 1).astype(F32)
        mask = (tk <= tq_ref[...]) & (tk < n_tok)
        update(cnew_ref[0], krnew_ref[0], cosn_ref[...], sinn_ref[...], mask)
        o_lat = (acc_s[...] / l_s[...]).astype(BF16)
        full = _dot(o_lat, wuv_ref[...])
        row = lax.broadcasted_iota(jnp.int32, (nq, MLA_V), 0)
        head = row % MLA_HEADS
        out = jnp.zeros((nq, MLA_V), F32)
        for h in range(MLA_HEADS):
            out = jnp.where(head == h, full[:, h * MLA_V:(h + 1) * MLA_V], out)
        o_ref[0] = out


def _mla_decode(page_flat, qbdn, qr, cnew, krnew, tq_rows, w, cos_p, sin_p, cos_n, sin_n,
                pool_c, pool_r, n_pages, page, n_tok):
    s, nq, _ = qr.shape
    pps = _pages_per_step(n_pages)
    assert 2 * page == KEY_TILE and page == LANES

    def page_spec(width):
        return lambda kk: pl.BlockSpec((1, page, width), lambda si, j, pt: (pt[si * n_pages + j * pps + kk], 0, 0))

    per_seq = lambda shape: pl.BlockSpec((1,) + shape, lambda si, j, pt: (si, 0, 0))
    cst = lambda a: pl.BlockSpec(a.shape, lambda si, j, pt: (0,) * a.ndim)
    tab = pl.BlockSpec((MLA_HALF, pps * page), lambda si, j, pt: (0, j))
    grid_spec = pltpu.PrefetchScalarGridSpec(
        num_scalar_prefetch=1,
        grid=(s, n_pages // pps),
        in_specs=[per_seq((nq, MLA_HEADS * MLA_NOPE)), per_seq((nq, MLA_ROPE)),
                  per_seq((LANES, MLA_KV_LORA)), per_seq((LANES, MLA_ROPE)), cst(tq_rows),
                  cst(w['w_ukT']), cst(w['w_uv']), cst(w['kgr']), tab, tab, cst(cos_n), cst(sin_n), cst(w['eye'])]
                 + [page_spec(MLA_KV_LORA)(kk) for kk in range(pps)]
                 + [page_spec(MLA_ROPE)(kk) for kk in range(pps)],
        out_specs=per_seq((nq, MLA_V)),
        scratch_shapes=[pltpu.VMEM((nq, MLA_KV_LORA), F32), pltpu.VMEM((nq, LANES), F32),
                        pltpu.VMEM((nq, LANES), F32), pltpu.VMEM((nq, MLA_KV_LORA), F32)],
    )
    return pl.pallas_call(
        functools.partial(_mla_decode_kernel, pps=pps, n_tok=float(n_tok)),
        grid_spec=grid_spec,
        out_shape=jax.ShapeDtypeStruct((s, nq, MLA_V), F32),
        compiler_params=pltpu.CompilerParams(dimension_semantics=("parallel", "arbitrary"),
                                             vmem_limit_bytes=VMEM_LIMIT_BYTES),
        name="mla_decode",
    )(page_flat, qbdn, qr, cnew, krnew, tq_rows, w['w_ukT'], w['w_uv'], w['kgr'], cos_p, sin_p, cos_n, sin_n,
      w['eye'], *([pool_c] * pps), *([pool_r] * pps))


def _out_mlp_kernel(x_ref, om_ref, ol_ref, gom_ref, gol_ref, wo_ref, ln2_ref, wup_ref, wdn_ref, y_ref):
    mixed = jnp.concatenate([_rms(om_ref[...], gom_ref[...]), _rms(ol_ref[...], gol_ref[...])], axis=-1)
    x2 = x_ref[...] + _dot(mixed.astype(BF16), wo_ref[...])
    hb = _rms(x2, ln2_ref[...]).astype(BF16)
    acc = x2
    for c in range(D_FF // D_MODEL):
        cols = slice(c * D_MODEL, (c + 1) * D_MODEL)
        u = jnp.maximum(_dot(hb, wup_ref[:, cols]), 0.0)
        acc = acc + _dot((u * u).astype(BF16), wdn_ref[cols, :])
    y_ref[...] = acc


def _out_mlp(x2d, om, ol, w):
    rows = x2d.shape[0]
    tm = min(ROW_TILE, rows)
    assert rows % tm == 0
    row_spec = lambda width: pl.BlockSpec((tm, width), lambda i: (i, 0))
    single = lambda a: pl.BlockSpec(a.shape, lambda i: (0,) * a.ndim, pipeline_mode=pl.Buffered(1))
    consts = [w['gom'], w['gol'], w['w_o'], w['ln2'], w['w_up'], w['w_down']]
    return pl.pallas_call(
        _out_mlp_kernel,
        grid=(rows // tm,),
        in_specs=[row_spec(D_MODEL), row_spec(MOBA_WIDTH), row_spec(MLA_WIDTH)] + [single(c) for c in consts],
        out_specs=row_spec(D_MODEL),
        out_shape=jax.ShapeDtypeStruct((rows, D_MODEL), F32),
        compiler_params=pltpu.CompilerParams(dimension_semantics=("parallel",), vmem_limit_bytes=VMEM_LIMIT_BYTES),
        name="out_mlp",
    )(x2d, om, ol, *consts)


def _rope_angles(pos):
    inv_freq = ROPE_BASE ** (-jnp.arange(MLA_HALF, dtype=F32) / MLA_HALF)
    ang = pos.astype(F32)[:, None] * inv_freq[None, :]
    return jnp.cos(ang), jnp.sin(ang)


def _query_rope_tables(pos):
    cos, sin = _rope_angles(pos)
    n = pos.shape[0]
    rc = jnp.concatenate([jnp.ones((n, MLA_NOPE), F32), cos, cos, jnp.zeros((n, LANES - MLA_QK), F32)], axis=1)
    zeros = lambda k: jnp.zeros((n, k), F32)
    rs1 = jnp.concatenate([zeros(MLA_NOPE + MLA_HALF), sin, zeros(LANES - MLA_QK)], axis=1)
    rs2 = jnp.concatenate([zeros(MLA_NOPE), -sin, zeros(LANES - MLA_NOPE - MLA_HALF)], axis=1)
    return rc, rs1, rs2


def _head_block(v):
    return jnp.concatenate([v, jnp.zeros((LANES - MLA_QK,), F32)])[None, :]


def _prep_weights(ln1_g, w_in, moba_q_g, moba_k_g, mla_q_lora_g, w_uq, mla_q_g, mla_kv_lora_g, w_uk, w_uv,
                  mla_k_g, out_g_moba, out_g_mla, w_o, ln2_g, w_up, w_down):
    w_uq_h = w_uq.reshape(MLA_Q_LORA, MLA_HEADS, MLA_QK)
    w_uq_h = jnp.pad(w_uq_h, ((0, 0), (0, 0), (0, LANES - MLA_QK))).reshape(MLA_Q_LORA, MLA_HEADS * LANES)
    kgn = jnp.concatenate([mla_k_g[:MLA_NOPE], jnp.ones((LANES - MLA_NOPE,), F32)])[None, :]
    return {
        'ln1': ln1_g[None, :],
        'w_main': w_in[:, :MAIN_WIDTH].astype(BF16),
        'w_kr': jnp.pad(w_in[:, MAIN_WIDTH:], ((0, 0), (0, LANES - MLA_ROPE))).astype(BF16),
        'gq': jnp.tile(moba_q_g, 2)[None, :],
        'gk': jnp.tile(moba_k_g, 2)[None, :],
        'gql': mla_q_lora_g[None, :],
        'w_uq': w_uq_h.astype(BF16),
        'gmq': _head_block(mla_q_g),
        'kgn': kgn,
        'gkv': mla_kv_lora_g[None, :],
        'w_ukT': w_uk.reshape(MLA_KV_LORA, MLA_HEADS * MLA_NOPE).T.astype(BF16),
        'w_uv': w_uv.reshape(MLA_KV_LORA, MLA_HEADS * MLA_V).astype(BF16),
        'kgr': mla_k_g[MLA_NOPE:][:, None],
        'eye': jnp.eye(MLA_ROPE, dtype=F32),
        'gom': out_g_moba[None, :],
        'gol': out_g_mla[None, :],
        'w_o': w_o.astype(BF16),
        'ln2': ln2_g[None, :],
        'w_up': w_up.astype(BF16),
        'w_down': w_down.astype(BF16),
    }


def _alibi_slopes():
    return jnp.exp2(-8.0 * jnp.arange(1, MOBA_HEADS + 1, dtype=F32) / MOBA_HEADS)


def _prompt_layer(x, w):
    b, t, _ = x.shape
    pos = jnp.arange(t, dtype=jnp.int32)
    x2d = x.reshape(b * t, D_MODEL)
    tm = min(ROW_TILE, b * t)
    assert t % tm == 0
    qm, km, vm, qmla, ckv, kr = _inproj(x2d, _query_rope_tables(pos), t // tm, w)
    o_moba = _moba_prompt(qm.reshape(b, t, -1), km.reshape(b, t, -1), vm.reshape(b, t, -1), _alibi_slopes())
    cos, sin = _rope_angles(pos)
    o_mla = _mla_prompt(qmla.reshape(b, t, -1), ckv.reshape(b, t, -1), kr.reshape(b, t, -1), w, cos.T, sin.T)
    y = _out_mlp(x2d, o_moba.reshape(b * t, -1), o_mla.reshape(b * t, -1), w)
    return (y.reshape(b, t, D_MODEL), km.reshape(b, t, MOBA_KV_HEADS, HEAD_DIM),
            vm.reshape(b, t, MOBA_KV_HEADS, HEAD_DIM), ckv.reshape(b, t, MLA_KV_LORA), kr.reshape(b, t, MLA_ROPE))


def _pad_rows(a, n):
    return jnp.pad(a, ((0, 0), (0, n - a.shape[1]), (0, 0)))


def _sample_layer(x, w, pool_k, pool_v, pool_c, pool_r, page_table):
    s, t, _ = x.shape
    n_pages = page_table.shape[1]
    page = pool_k.shape[1]
    past_len = n_pages * page
    nq = t * MOBA_HEADS
    x2d = x.reshape(s * t, D_MODEL)
    tm = min(ROW_TILE, s * t)
    assert tm % t == 0
    pos_new = past_len + jnp.arange(t, dtype=jnp.int32)
    qm, km, vm, qmla, ckv, kr = _inproj(x2d, _query_rope_tables(jnp.tile(pos_new, tm // t)), 1, w)

    page_flat = page_table.reshape(-1)
    row = jnp.arange(nq)
    tq_rows = jnp.broadcast_to((row // MOBA_HEADS).astype(F32)[:, None], (nq, LANES))
    slope_rows = jnp.broadcast_to(_alibi_slopes()[row % MOBA_HEADS][:, None], (nq, LANES))

    kv_onehot = jax.nn.one_hot(jnp.arange(MOBA_HEADS) // (MOBA_HEADS // MOBA_KV_HEADS), MOBA_KV_HEADS, dtype=F32)
    qbd = (qm.reshape(s, t, MOBA_HEADS, 1, HEAD_DIM) * kv_onehot[None, None, :, :, None]).reshape(s, nq, MOBA_KV_WIDTH)
    n_new = -(-t // (2 * SUBLANES)) * (2 * SUBLANES)
    o_moba = _moba_decode(page_flat, qbd, _pad_rows(km.reshape(s, t, -1), n_new), _pad_rows(vm.reshape(s, t, -1), n_new),
                          slope_rows, tq_rows, pool_k.reshape(-1, page, MOBA_KV_WIDTH),
                          pool_v.reshape(-1, page, MOBA_KV_WIDTH), n_pages, page)

    qh = qmla.reshape(s, t, MLA_HEADS, LANES)
    head_eye = jnp.eye(MLA_HEADS, dtype=F32)
    qbdn = (qh[..., None, :MLA_NOPE] * head_eye[None, None, :, :, None]).reshape(s, nq, MLA_HEADS * MLA_NOPE)
    qr = qh[..., MLA_NOPE:MLA_QK].reshape(s, nq, MLA_ROPE)
    cos_p, sin_p = _rope_angles(jnp.arange(past_len, dtype=jnp.int32))
    cos_n, sin_n = _rope_angles(past_len + jnp.arange(LANES, dtype=jnp.int32))
    o_mla = _mla_decode(page_flat, qbdn, qr, _pad_rows(ckv.reshape(s, t, -1), LANES), _pad_rows(kr.reshape(s, t, -1), LANES),
                        tq_rows, w, cos_p.T, sin_p.T, cos_n.T, sin_n.T, pool_c, pool_r, n_pages, page, t)

    y = _out_mlp(x2d, o_moba.reshape(s * t, MOBA_WIDTH), o_mla.reshape(s * t, MLA_WIDTH), w)
    return (y.reshape(s, t, D_MODEL), km.reshape(s, t, MOBA_KV_HEADS, HEAD_DIM),
            vm.reshape(s, t, MOBA_KV_HEADS, HEAD_DIM), ckv.reshape(s, t, MLA_KV_LORA), kr.reshape(s, t, MLA_ROPE))


def kernel(x_prompt, x_sample, cache_moba_k, cache_moba_v, cache_mla_ckv, cache_mla_krope, page_table, ln1_g, w_in,
           moba_q_g, moba_k_g, mla_q_lora_g, w_uq, mla_q_g, mla_kv_lora_g, w_uk, w_uv, mla_k_g, out_g_moba,
           out_g_mla, w_o, ln2_g, w_up, w_down):
    depth = w_in.shape[0]
    layer_weights = (ln1_g, w_in, moba_q_g, moba_k_g, mla_q_lora_g, w_uq, mla_q_g, mla_kv_lora_g, w_uk, w_uv,
                     mla_k_g, out_g_moba, out_g_mla, w_o, ln2_g, w_up, w_down)
    y_p, y_s = x_prompt, x_sample
    outs_p, outs_s = [], []
    for layer in range(depth):
        w = _prep_weights(*(a[layer] for a in layer_weights))
        y_p, *new_p = _prompt_layer(y_p, w)
        y_s, *new_s = _sample_layer(y_s, w, cache_moba_k[layer], cache_moba_v[layer], cache_mla_ckv[layer],
                                    cache_mla_krope[layer], page_table)
        outs_p.append(new_p)
        outs_s.append(new_s)
    stack = lambda outs, k: jnp.stack([o[k] for o in outs])
    return (y_p, y_s, stack(outs_p, 0), stack(outs_p, 1), stack(outs_p, 2), stack(outs_p, 3),
            stack(outs_s, 0), stack(outs_s, 1), stack(outs_s, 2), stack(outs_s, 3))
```

```python
import functools

import jax
import jax.numpy as jnp
from jax import lax
from jax.experimental import pallas as pl
from jax.experimental.pallas import tpu as pltpu

F32 = jnp.float32
BF16 = jnp.bfloat16

D_MODEL = 1024
HEAD_DIM = 64
MOBA_HEADS = 8
MOBA_KV_HEADS = 4
MOBA_WIDTH = MOBA_HEADS * HEAD_DIM
MOBA_KV_WIDTH = MOBA_KV_HEADS * HEAD_DIM
MOBA_BLOCK = 256
MOBA_TOPK = 3
MLA_HEADS = 8
MLA_V = 64
MLA_WIDTH = MLA_HEADS * MLA_V
MLA_NOPE = 64
MLA_ROPE = 32
MLA_HALF = MLA_ROPE // 2
MLA_QK = MLA_NOPE + MLA_ROPE
MLA_Q_LORA = 384
MLA_KV_LORA = 128
ROPE_BASE = 10000.0
D_FF = 4 * D_MODEL
EPS = 1e-6
MAIN_WIDTH = MOBA_WIDTH + 2 * MOBA_KV_WIDTH + MLA_Q_LORA + MLA_KV_LORA

LANES = 128
SUBLANES = 8
VMEM_LIMIT_BYTES = 56 * 1024 * 1024

ROW_TILE = 512
Q_TILE = MOBA_BLOCK
KEY_TILE = 256
MLA_DECODE_SUB_PAGES = 4
NEG = -1e30

_NT = (((1,), (1,)), ((), ()))


def _dot(a, b):
    return jnp.dot(a, b, preferred_element_type=F32)


def _dot_nt(a, b):
    return lax.dot_general(a, b, _NT, preferred_element_type=F32)


def _split_bf16(a):
    hi = a.astype(BF16)
    lo = (a - hi.astype(F32)).astype(BF16)
    return hi, lo


def _dot_nt_f32(a, b):
    ah, al = _split_bf16(a)
    bh, bl = _split_bf16(b)
    return _dot_nt(ah, bh) + _dot_nt(ah, bl) + _dot_nt(al, bh)


def _rms(x, g):
    return x * lax.rsqrt(jnp.mean(x * x, axis=-1, keepdims=True) + EPS) * g


def _const_spec(shape):
    nd = len(shape)
    return pl.BlockSpec(shape, lambda *_: (0,) * nd)


def _inproj_kernel(x_ref, ln1_ref, wmain_ref, wkr_ref, gq_ref, gk_ref, gql_ref, wuq_ref, gmq_ref,
                   kgn_ref, gkv_ref, rc_ref, rs1_ref, rs2_ref,
                   qm_ref, km_ref, vm_ref, qmla_ref, ckv_ref, kr_ref):
    x = x_ref[...]
    hb = _rms(x, ln1_ref[...]).astype(BF16)
    z = _dot(hb, wmain_ref[...])
    zk = _dot(hb, wkr_ref[...])
    tm = x.shape[0]
    lane = lax.broadcasted_iota(jnp.int32, (tm, LANES), 1)
    low = lane < HEAD_DIM

    def pair_norm(blk, g):
        sq = blk * blk
        lo = jnp.sum(jnp.where(low, sq, 0.0), axis=-1, keepdims=True)
        hi = jnp.sum(jnp.where(low, 0.0, sq), axis=-1, keepdims=True)
        inv = jnp.where(low, lax.rsqrt(lo / HEAD_DIM + EPS), lax.rsqrt(hi / HEAD_DIM + EPS))
        return blk * inv * g

    for j in range(MOBA_WIDTH // LANES):
        qm_ref[:, j * LANES:(j + 1) * LANES] = pair_norm(z[:, j * LANES:(j + 1) * LANES], gq_ref[...])
    for j in range(MOBA_KV_WIDTH // LANES):
        o = MOBA_WIDTH + j * LANES
        km_ref[:, j * LANES:(j + 1) * LANES] = pair_norm(z[:, o:o + LANES], gk_ref[...])
    o = MOBA_WIDTH + MOBA_KV_WIDTH
    vm_ref[...] = z[:, o:o + MOBA_KV_WIDTH]
    o += MOBA_KV_WIDTH
    cqn = _rms(z[:, o:o + MLA_Q_LORA], gql_ref[...]).astype(BF16)
    o += MLA_Q_LORA
    ckv_ref[...] = _rms(z[:, o:o + MLA_KV_LORA], gkv_ref[...])
    kr_ref[...] = zk[:, :MLA_ROPE]

    qh = _dot(cqn, wuq_ref[...])
    rc, rs1, rs2 = rc_ref[...], rs1_ref[...], rs2_ref[...]
    gain = gmq_ref[...]
    kgn = kgn_ref[...]
    for h in range(MLA_HEADS):
        blk = qh[:, h * LANES:(h + 1) * LANES]
        ss = jnp.sum(blk * blk, axis=-1, keepdims=True)
        y = blk * lax.rsqrt(ss / MLA_QK + EPS) * gain
        y = y * rc + pltpu.roll(y, MLA_HALF, axis=1) * rs1 + pltpu.roll(y, LANES - MLA_HALF, axis=1) * rs2
        qmla_ref[:, h * LANES:(h + 1) * LANES] = y * kgn


def _inproj(x2d, tabs, n_tab, w):
    rows = x2d.shape[0]
    tm = min(ROW_TILE, rows)
    assert rows % tm == 0 and tm % SUBLANES == 0
    row_spec = lambda width: pl.BlockSpec((tm, width), lambda i: (i, 0))
    tab_spec = pl.BlockSpec((tm, LANES), lambda i: (i % n_tab, 0))
    consts = [w['ln1'], w['w_main'], w['w_kr'], w['gq'], w['gk'], w['gql'], w['w_uq'], w['gmq'], w['kgn'], w['gkv']]
    out_widths = (MOBA_WIDTH, MOBA_KV_WIDTH, MOBA_KV_WIDTH, MLA_HEADS * LANES, MLA_KV_LORA, MLA_ROPE)
    return pl.pallas_call(
        _inproj_kernel,
        grid=(rows // tm,),
        in_specs=[row_spec(D_MODEL)] + [_const_spec(c.shape) for c in consts] + [tab_spec] * 3,
        out_specs=[row_spec(wd) for wd in out_widths],
        out_shape=[jax.ShapeDtypeStruct((rows, wd), F32) for wd in out_widths],
        compiler_params=pltpu.CompilerParams(dimension_semantics=("parallel",), vmem_limit_bytes=VMEM_LIMIT_BYTES),
        name="inproj",
    )(x2d, *consts, *tabs)


def _moba_prompt_kernel(q_ref, k_ref, v_ref, slope_ref, o_ref, kmean_ref, *, nb):
    i = pl.program_id(1)
    nbp = kmean_ref.shape[0]

    @pl.when(i == 0)
    def _():
        kmean_ref[...] = jnp.zeros_like(kmean_ref)
        for n in range(nb):
            kmean_ref[n:n + 1, :] = jnp.mean(k_ref[0, n * MOBA_BLOCK:(n + 1) * MOBA_BLOCK, :], axis=0, keepdims=True)

    scale = HEAD_DIM ** -0.5
    col = lax.broadcasted_iota(jnp.int32, (Q_TILE, nbp), 1)
    past = col < i
    qrow = lax.broadcasted_iota(jnp.int32, (Q_TILE, KEY_TILE), 0)
    kcol = lax.broadcasted_iota(jnp.int32, (Q_TILE, KEY_TILE), 1)
    rel = (qrow - kcol).astype(F32)

    for h in range(MOBA_HEADS):
        g = h // (MOBA_HEADS // MOBA_KV_HEADS)
        qh = q_ref[0, :, h * HEAD_DIM:(h + 1) * HEAD_DIM]
        gate = _dot_nt_f32(qh, kmean_ref[:, g * HEAD_DIM:(g + 1) * HEAD_DIM])
        sel = jnp.zeros((Q_TILE, nbp), F32)
        for n in range(nb):
            gn = gate[:, n:n + 1]
            beats = ((gate > gn) | ((gate == gn) & (col < n))) & past
            rank = jnp.sum(beats.astype(F32), axis=-1, keepdims=True)
            sel = jnp.where((col == n) & (rank < MOBA_TOPK) & past, 1.0, sel)
        sel = jnp.where(col == i, 1.0, sel)
        qb = qh.astype(BF16)
        slope = slope_ref[h]

        def body(n, carry):
            m, l, acc = carry
            start = pl.multiple_of(n * MOBA_BLOCK, MOBA_BLOCK)
            kb = k_ref[0, pl.ds(start, MOBA_BLOCK), g * HEAD_DIM:(g + 1) * HEAD_DIM].astype(BF16)
            vb = v_ref[0, pl.ds(start, MOBA_BLOCK), g * HEAD_DIM:(g + 1) * HEAD_DIM].astype(BF16)
            dist = rel + ((i - n) * MOBA_BLOCK).astype(F32)
            s = _dot_nt(qb, kb) * scale - slope * dist
            seln = jnp.sum(jnp.where(col == n, sel, 0.0), axis=-1, keepdims=True)
            mask = (seln > 0.5) & (dist >= 0.0)
            s = jnp.where(mask, s, NEG)
            m_new = jnp.maximum(m, jnp.max(s, axis=-1, keepdims=True))
            alpha = jnp.exp(m - m_new)
            p = jnp.where(mask, jnp.exp(s - m_new), 0.0)
            l = alpha * l + jnp.sum(p, axis=-1, keepdims=True)
            acc = alpha * acc + _dot(p.astype(BF16), vb)
            return m_new, l, acc

        init = (jnp.full((Q_TILE, 1), NEG, F32), jnp.zeros((Q_TILE, 1), F32), jnp.zeros((Q_TILE, HEAD_DIM), F32))
        _, l, acc = lax.fori_loop(0, i + 1, body, init)
        o_ref[0, :, h * HEAD_DIM:(h + 1) * HEAD_DIM] = acc / l


def _moba_prompt(qm, km, vm, slopes):
    b, t, _ = qm.shape
    assert t % MOBA_BLOCK == 0
    nb = t // MOBA_BLOCK
    assert nb >= MOBA_TOPK
    nbp = -(-nb // SUBLANES) * SUBLANES
    return pl.pallas_call(
        functools.partial(_moba_prompt_kernel, nb=nb),
        grid=(b, nb),
        in_specs=[pl.BlockSpec((1, Q_TILE, MOBA_WIDTH), lambda bi, i: (bi, i, 0)),
                  pl.BlockSpec((1, t, MOBA_KV_WIDTH), lambda bi, i: (bi, 0, 0)),
                  pl.BlockSpec((1, t, MOBA_KV_WIDTH), lambda bi, i: (bi, 0, 0)),
                  pl.BlockSpec(memory_space=pltpu.SMEM)],
        out_specs=pl.BlockSpec((1, Q_TILE, MOBA_WIDTH), lambda bi, i: (bi, i, 0)),
        out_shape=jax.ShapeDtypeStruct((b, t, MOBA_WIDTH), F32),
        scratch_shapes=[pltpu.VMEM((nbp, MOBA_KV_WIDTH), F32)],
        compiler_params=pltpu.CompilerParams(dimension_semantics=("parallel", "arbitrary"),
                                             vmem_limit_bytes=VMEM_LIMIT_BYTES),
        name="moba_prompt",
    )(qm, km, vm, slopes)


def _mla_key_tile(c_bf, krT, wukT, kgr, cos, sin):
    kT = _dot_nt(wukT, c_bf)
    n_keys = kT.shape[1]
    kn = kT[:MLA_HEADS * MLA_NOPE]
    ssq = jnp.sum((kn * kn).reshape(MLA_HEADS, MLA_NOPE, n_keys), axis=1)
    krsq = jnp.sum(krT * krT, axis=0, keepdims=True)
    inv = lax.rsqrt((ssq + krsq) / MLA_QK + EPS) * (MLA_QK ** -0.5)
    x = krT * kgr
    x1, x2 = x[:MLA_HALF], x[MLA_HALF:]
    rot = jnp.concatenate([x1 * cos - x2 * sin, x1 * sin + x2 * cos], axis=0)
    return kT, rot, inv


def _mla_prompt_kernel(q_ref, c_ref, krT_ref, wukT_ref, wuv_ref, kgr_ref, cos_ref, sin_ref,
                       o_ref, khat_ref, vhat_ref, *, nb):
    i = pl.program_id(1)

    @pl.when(i == 0)
    def _():
        for n in range(nb):
            rows = slice(n * KEY_TILE, (n + 1) * KEY_TILE)
            cb = c_ref[0, rows, :].astype(BF16)
            kT, rot, inv = _mla_key_tile(cb, krT_ref[0, :, rows], wukT_ref[...], kgr_ref[...],
                                         cos_ref[:, rows], sin_ref[:, rows])
            pad = jnp.zeros((LANES - MLA_QK, KEY_TILE), F32)
            for h in range(MLA_HEADS):
                sc = inv[h:h + 1]
                kh = jnp.concatenate([kT[h * MLA_NOPE:(h + 1) * MLA_NOPE] * sc, rot * sc, pad], axis=0)
                khat_ref[h, n] = kh.astype(BF16)
            vhat_ref[n] = _dot(cb, wuv_ref[...]).astype(BF16)

    qrow = lax.broadcasted_iota(jnp.int32, (Q_TILE, KEY_TILE), 0)
    kcol = lax.broadcasted_iota(jnp.int32, (Q_TILE, KEY_TILE), 1)

    for h in range(MLA_HEADS):
        qb = q_ref[0, :, h * LANES:(h + 1) * LANES].astype(BF16)

        def body(n, carry):
            m, l, acc = carry
            s = _dot(qb, khat_ref[h, n])
            mask = (kcol + n * KEY_TILE) <= (qrow + i * KEY_TILE)
            s = jnp.where(mask, s, NEG)
            m_new = jnp.maximum(m, jnp.max(s, axis=-1, keepdims=True))
            alpha = jnp.exp(m - m_new)
            p = jnp.where(mask, jnp.exp(s - m_new), 0.0)
            l = alpha * l + jnp.sum(p, axis=-1, keepdims=True)
            acc = alpha * acc + _dot(p.astype(BF16), vhat_ref[n, :, h * MLA_V:(h + 1) * MLA_V])
            return m_new, l, acc

        init = (jnp.full((Q_TILE, 1), NEG, F32), jnp.zeros((Q_TILE, 1), F32), jnp.zeros((Q_TILE, MLA_V), F32))
        _, l, acc = lax.fori_loop(0, i + 1, body, init)
        o_ref[0, :, h * MLA_V:(h + 1) * MLA_V] = acc / l


def _mla_prompt(qmla, ckv, krT, w, cos_t, sin_t):
    b, t, _ = qmla.shape
    assert t % KEY_TILE == 0
    nb = t // KEY_TILE
    consts = [w['w_ukT'], w['w_uv'], w['kgr'], cos_t, sin_t]
    return pl.pallas_call(
        functools.partial(_mla_prompt_kernel, nb=nb),
        grid=(b, nb),
        in_specs=[pl.BlockSpec((1, Q_TILE, MLA_HEADS * LANES), lambda bi, i: (bi, i, 0)),
                  pl.BlockSpec((1, t, MLA_KV_LORA), lambda bi, i: (bi, 0, 0)),
                  pl.BlockSpec((1, MLA_ROPE, t), lambda bi, i: (bi, 0, 0))]
                 + [_const_spec(c.shape) for c in consts],
        out_specs=pl.BlockSpec((1, Q_TILE, MLA_WIDTH), lambda bi, i: (bi, i, 0)),
        out_shape=jax.ShapeDtypeStruct((b, t, MLA_WIDTH), F32),
        scratch_shapes=[pltpu.VMEM((MLA_HEADS, nb, LANES, KEY_TILE), BF16),
                        pltpu.VMEM((nb, KEY_TILE, MLA_WIDTH), BF16)],
        compiler_params=pltpu.CompilerParams(dimension_semantics=("parallel", "arbitrary"),
                                             vmem_limit_bytes=VMEM_LIMIT_BYTES),
        name="mla_prompt",
    )(qmla, ckv, krT, *consts)


def _moba_decode_kernel(pt_ref, qbd_ref, knew_ref, vnew_ref, slope_ref, tq_ref, *rest, pps, past_len):
    del pt_ref
    kpages, vpages = rest[:pps], rest[pps:2 * pps]
    o_ref = rest[2 * pps]
    gate_s, m_s, l_s, o_s = rest[2 * pps + 1:]
    j = pl.program_id(1)
    nblk = gate_s.shape[0]
    nq = qbd_ref.shape[1]
    scale = HEAD_DIM ** -0.5
    bps = pps // 2

    qbd = qbd_ref[0]
    qb = qbd.astype(BF16)
    slope = slope_ref[...]
    tq = tq_ref[...]
    slope2 = jnp.concatenate([slope, slope], axis=1)
    tq2 = jnp.concatenate([tq, tq], axis=1)
    lane2 = lax.broadcasted_iota(jnp.int32, (nq, MOBA_BLOCK), 1).astype(F32)

    def page_t(ref):
        return ref[0].reshape(MOBA_KV_WIDTH, ref.shape[3]).astype(BF16)

    for bb in range(bps):
        ktb = jnp.concatenate([page_t(kpages[2 * bb]), page_t(kpages[2 * bb + 1])], axis=1)
        vtb = jnp.concatenate([page_t(vpages[2 * bb]), page_t(vpages[2 * bb + 1])], axis=1)
        blk = j * bps + bb
        dist = (past_len + tq2) - ((blk * MOBA_BLOCK).astype(F32) + lane2)
        qk = _dot(qb, ktb)
        gate = jnp.sum(qk, axis=-1, keepdims=True) / MOBA_BLOCK
        s = qk * scale - slope2 * dist
        m = jnp.max(s, axis=-1, keepdims=True)
        p = jnp.exp(s - m)
        l = jnp.sum(p, axis=-1, keepdims=True)
        gate_s[blk] = jnp.broadcast_to(gate, (nq, LANES))
        m_s[blk] = jnp.broadcast_to(m, (nq, LANES))
        l_s[blk] = jnp.broadcast_to(l, (nq, LANES))
        o_s[blk] = _dot_nt(p.astype(BF16), vtb)

    @pl.when(j == pl.num_programs(1) - 1)
    def _():
        gates = gate_s[...]
        bidx = lax.broadcasted_iota(jnp.int32, gates.shape, 0)
        picked = jnp.zeros(gates.shape, F32)
        for _ in range(min(MOBA_TOPK, nblk)):
            mx = jnp.max(gates, axis=0, keepdims=True)
            first = jnp.min(jnp.where(gates == mx, bidx, nblk), axis=0, keepdims=True)
            hit = bidx == first
            picked = jnp.where(hit, 1.0, picked)
            gates = jnp.where(hit, -jnp.inf, gates)
        sel = picked > 0.5

        n_new = knew_ref.shape[1]
        tk = lax.broadcasted_iota(jnp.int32, (nq, n_new), 1).astype(F32)
        tq_n = tq[:, :n_new]
        dist_o = tq_n - tk
        s_o = _dot_nt(qb, knew_ref[0].astype(BF16)) * scale - slope[:, :n_new] * dist_o
        mask_o = dist_o >= 0.0
        s_o = jnp.where(mask_o, s_o, NEG)
        m_o = jnp.broadcast_to(jnp.max(s_o, axis=-1, keepdims=True), (nq, LANES))

        m_all = m_s[...]
        m_fin = jnp.maximum(jnp.max(jnp.where(sel, m_all, NEG), axis=0), m_o)
        wgt = jnp.exp(jnp.where(sel, m_all - m_fin[None], -jnp.inf))
        p_o = jnp.where(mask_o, jnp.exp(s_o - m_fin[:, :n_new]), 0.0)
        l_fin = jnp.sum(wgt * l_s[...], axis=0) + jnp.sum(p_o, axis=-1, keepdims=True)
        wgt2 = jnp.concatenate([wgt, wgt], axis=2)
        o_fin = jnp.sum(wgt2 * o_s[...], axis=0) + _dot(p_o.astype(BF16), vnew_ref[0].astype(BF16))
        o_fin = o_fin / jnp.concatenate([l_fin, l_fin], axis=1)

        row = lax.broadcasted_iota(jnp.int32, (nq, HEAD_DIM), 0)
        grp = (row % MOBA_HEADS) // (MOBA_HEADS // MOBA_KV_HEADS)
        out = jnp.zeros((nq, HEAD_DIM), F32)
        for g in range(MOBA_KV_HEADS):
            out = jnp.where(grp == g, o_fin[:, g * HEAD_DIM:(g + 1) * HEAD_DIM], out)
        o_ref[0] = out


def _pages_per_step(n_pages):
    pps = min(16, n_pages)
    assert n_pages % pps == 0 and pps % 2 == 0
    return pps


def _moba_decode(page_flat, qbd, knew, vnew, slope_rows, tq_rows, pool_kt, pool_vt, n_pages, page):
    s, nq, _ = qbd.shape
    assert 2 * page == MOBA_BLOCK and pool_kt.shape[1:] == (MOBA_KV_HEADS, HEAD_DIM, page)
    pps = _pages_per_step(n_pages)
    nblk = n_pages // 2
    n_new = knew.shape[1]

    def page_spec(kk):
        return pl.BlockSpec((1, MOBA_KV_HEADS, HEAD_DIM, page),
                            lambda si, j, pt: (pt[si * n_pages + j * pps + kk], 0, 0, 0))

    per_seq = lambda shape: pl.BlockSpec((1,) + shape, lambda si, j, pt: (si, 0, 0))
    const2 = pl.BlockSpec((nq, LANES), lambda si, j, pt: (0, 0))
    grid_spec = pltpu.PrefetchScalarGridSpec(
        num_scalar_prefetch=1,
        grid=(s, n_pages // pps),
        in_specs=[per_seq((nq, MOBA_KV_WIDTH)), per_seq((n_new, MOBA_KV_WIDTH)), per_seq((n_new, MOBA_KV_WIDTH)),
                  const2, const2]
                 + [page_spec(kk) for kk in range(pps)] * 2,
        out_specs=per_seq((nq, HEAD_DIM)),
        scratch_shapes=[pltpu.VMEM((nblk, nq, LANES), F32)] * 3 + [pltpu.VMEM((nblk, nq, MOBA_KV_WIDTH), F32)],
    )
    return pl.pallas_call(
        functools.partial(_moba_decode_kernel, pps=pps, past_len=float(n_pages * page)),
        grid_spec=grid_spec,
        out_shape=jax.ShapeDtypeStruct((s, nq, HEAD_DIM), F32),
        compiler_params=pltpu.CompilerParams(dimension_semantics=("parallel", "arbitrary"),
                                             vmem_limit_bytes=VMEM_LIMIT_BYTES),
        name="moba_decode",
    )(page_flat, qbd, knew, vnew, slope_rows, tq_rows, *([pool_kt] * pps), *([pool_vt] * pps))


def _mla_decode_kernel(pt_ref, qbdn_ref, qr_ref, cnew_ref, krnew_ref, tq_ref, wukT_ref, wuv_ref, kgr_ref,
                       cos_ref, sin_ref, cosn_ref, sinn_ref, *rest, pps, sub, n_tok):
    del pt_ref
    cpages, rpages = rest[:pps], rest[pps:2 * pps]
    o_ref = rest[2 * pps]
    lhs_s, m_s, l_s, acc_s = rest[2 * pps + 1:]
    j = pl.program_id(1)
    nq = qr_ref.shape[1]
    page = cpages[0].shape[1]
    n_hd = MLA_HEADS * MLA_NOPE

    @pl.when(j == 0)
    def _():
        lhs_s[:n_hd, :] = wukT_ref[...]
        lhs_s[n_hd:, :] = _dot(qbdn_ref[0].astype(BF16), wukT_ref[...]).astype(BF16)

    qr = qr_ref[0].astype(BF16)

    def partial_softmax(cb, krT, cos, sin, mask):
        kT, rot, inv = _mla_key_tile(cb, krT, lhs_s[...], kgr_ref[...], cos, sin)
        inv_rows = jnp.concatenate([inv] * (nq // MLA_HEADS), axis=0)
        s = (kT[n_hd:] + _dot(qr, rot.astype(BF16))) * inv_rows
        if mask is not None:
            s = jnp.where(mask, s, NEG)
        m = jnp.max(s, axis=-1, keepdims=True)
        p = jnp.exp(s - m)
        if mask is not None:
            p = jnp.where(mask, p, 0.0)
        return m, jnp.sum(p, axis=-1, keepdims=True), _dot(p.astype(BF16), cb)

    for sc in range(pps // sub):
        pages = range(sc * sub, (sc + 1) * sub)
        cb = jnp.concatenate([cpages[k][0] for k in pages], axis=0).astype(BF16)
        krT = jnp.concatenate([rpages[k][0] for k in pages], axis=1)
        cols = slice(sc * sub * page, (sc + 1) * sub * page)
        m, l, acc = partial_softmax(cb, krT, cos_ref[:, cols], sin_ref[:, cols], None)
        slot = j * (pps // sub) + sc
        m_s[slot] = jnp.broadcast_to(m, (nq, LANES))
        l_s[slot] = jnp.broadcast_to(l, (nq, LANES))
        acc_s[slot] = acc

    @pl.when(j == pl.num_programs(1) - 1)
    def _():
        tk = lax.broadcasted_iota(jnp.int32, (nq, LANES), 1).astype(F32)
        mask = (tk <= tq_ref[...]) & (tk < n_tok)
        m_n, l_n, acc_n = partial_softmax(cnew_ref[0].astype(BF16), krnew_ref[0], cosn_ref[...], sinn_ref[...], mask)
        m_all = m_s[...]
        m_fin = jnp.maximum(jnp.max(m_all, axis=0), m_n)
        wgt = jnp.exp(m_all - m_fin[None])
        w_n = jnp.exp(m_n - m_fin)
        l_fin = jnp.sum(wgt * l_s[...], axis=0) + w_n * l_n
        a_fin = jnp.sum(wgt * acc_s[...], axis=0) + w_n * acc_n
        o_lat = (a_fin / l_fin).astype(BF16)
        full = _dot(o_lat, wuv_ref[...])
        row = lax.broadcasted_iota(jnp.int32, (nq, MLA_V), 0)
        head = row % MLA_HEADS
        out = jnp.zeros((nq, MLA_V), F32)
        for h in range(MLA_HEADS):
            out = jnp.where(head == h, full[:, h * MLA_V:(h + 1) * MLA_V], out)
        o_ref[0] = out


def _mla_decode(page_flat, qbdn, qr, cnew, krnew_t, tq_rows, w, cos_p, sin_p, cos_n, sin_n,
                pool_c, pool_rt, n_pages, page, n_tok):
    s, nq, _ = qr.shape
    pps = _pages_per_step(n_pages)
    sub = min(MLA_DECODE_SUB_PAGES, pps)
    assert page == LANES and pps % sub == 0
    n_slots = n_pages // sub

    def c_spec(kk):
        return pl.BlockSpec((1, page, MLA_KV_LORA), lambda si, j, pt: (pt[si * n_pages + j * pps + kk], 0, 0))

    def r_spec(kk):
        return pl.BlockSpec((1, MLA_ROPE, page), lambda si, j, pt: (pt[si * n_pages + j * pps + kk], 0, 0))

    per_seq = lambda shape: pl.BlockSpec((1,) + shape, lambda si, j, pt: (si, 0, 0))
    cst = lambda a: pl.BlockSpec(a.shape, lambda si, j, pt: (0,) * a.ndim)
    tab = pl.BlockSpec((MLA_HALF, pps * page), lambda si, j, pt: (0, j))
    n_lhs = MLA_HEADS * MLA_NOPE + nq
    grid_spec = pltpu.PrefetchScalarGridSpec(
        num_scalar_prefetch=1,
        grid=(s, n_pages // pps),
        in_specs=[per_seq((nq, MLA_HEADS * MLA_NOPE)), per_seq((nq, MLA_ROPE)),
                  per_seq((LANES, MLA_KV_LORA)), per_seq((MLA_ROPE, LANES)), cst(tq_rows),
                  cst(w['w_ukT']), cst(w['w_uv']), cst(w['kgr']), tab, tab, cst(cos_n), cst(sin_n)]
                 + [c_spec(kk) for kk in range(pps)] + [r_spec(kk) for kk in range(pps)],
        out_specs=per_seq((nq, MLA_V)),
        scratch_shapes=[pltpu.VMEM((n_lhs, MLA_KV_LORA), BF16), pltpu.VMEM((n_slots, nq, LANES), F32),
                        pltpu.VMEM((n_slots, nq, LANES), F32), pltpu.VMEM((n_slots, nq, MLA_KV_LORA), F32)],
    )
    return pl.pallas_call(
        functools.partial(_mla_decode_kernel, pps=pps, sub=sub, n_tok=float(n_tok)),
        grid_spec=grid_spec,
        out_shape=jax.ShapeDtypeStruct((s, nq, MLA_V), F32),
        compiler_params=pltpu.CompilerParams(dimension_semantics=("parallel", "arbitrary"),
                                             vmem_limit_bytes=VMEM_LIMIT_BYTES),
        name="mla_decode",
    )(page_flat, qbdn, qr, cnew, krnew_t, tq_rows, w['w_ukT'], w['w_uv'], w['kgr'], cos_p, sin_p, cos_n, sin_n,
      *([pool_c] * pps), *([pool_rt] * pps))


def _out_mlp_kernel(x_ref, om_ref, ol_ref, gom_ref, gol_ref, wo_ref, ln2_ref, wup_ref, wdn_ref, y_ref):
    mixed = jnp.concatenate([_rms(om_ref[...], gom_ref[...]), _rms(ol_ref[...], gol_ref[...])], axis=-1)
    x2 = x_ref[...] + _dot(mixed.astype(BF16), wo_ref[...])
    hb = _rms(x2, ln2_ref[...]).astype(BF16)
    acc = x2
    for c in range(D_FF // D_MODEL):
        cols = slice(c * D_MODEL, (c + 1) * D_MODEL)
        u = jnp.maximum(_dot(hb, wup_ref[:, cols]), 0.0)
        acc = acc + _dot((u * u).astype(BF16), wdn_ref[cols, :])
    y_ref[...] = acc


def _out_mlp(x2d, om, ol, w):
    rows = x2d.shape[0]
    tm = min(ROW_TILE, rows)
    assert rows % tm == 0
    row_spec = lambda width: pl.BlockSpec((tm, width), lambda i: (i, 0))
    single = lambda a: pl.BlockSpec(a.shape, lambda i: (0,) * a.ndim, pipeline_mode=pl.Buffered(1))
    consts = [w['gom'], w['gol'], w['w_o'], w['ln2'], w['w_up'], w['w_down']]
    return pl.pallas_call(
        _out_mlp_kernel,
        grid=(rows // tm,),
        in_specs=[row_spec(D_MODEL), row_spec(MOBA_WIDTH), row_spec(MLA_WIDTH)] + [single(c) for c in consts],
        out_specs=row_spec(D_MODEL),
        out_shape=jax.ShapeDtypeStruct((rows, D_MODEL), F32),
        compiler_params=pltpu.CompilerParams(dimension_semantics=("parallel",), vmem_limit_bytes=VMEM_LIMIT_BYTES),
        name="out_mlp",
    )(x2d, om, ol, *consts)


def _rope_angles(pos):
    inv_freq = ROPE_BASE ** (-jnp.arange(MLA_HALF, dtype=F32) / MLA_HALF)
    ang = pos.astype(F32)[:, None] * inv_freq[None, :]
    return jnp.cos(ang), jnp.sin(ang)


def _query_rope_tables(pos):
    cos, sin = _rope_angles(pos)
    n = pos.shape[0]
    rc = jnp.concatenate([jnp.ones((n, MLA_NOPE), F32), cos, cos, jnp.zeros((n, LANES - MLA_QK), F32)], axis=1)
    zeros = lambda k: jnp.zeros((n, k), F32)
    rs1 = jnp.concatenate([zeros(MLA_NOPE + MLA_HALF), sin, zeros(LANES - MLA_QK)], axis=1)
    rs2 = jnp.concatenate([zeros(MLA_NOPE), -sin, zeros(LANES - MLA_NOPE - MLA_HALF)], axis=1)
    return rc, rs1, rs2


def _head_block(v):
    return jnp.concatenate([v, jnp.zeros((LANES - MLA_QK,), F32)])[None, :]


def _prep_weights(ln1_g, w_in, moba_q_g, moba_k_g, mla_q_lora_g, w_uq, mla_q_g, mla_kv_lora_g, w_uk, w_uv,
                  mla_k_g, out_g_moba, out_g_mla, w_o, ln2_g, w_up, w_down):
    w_uq_h = w_uq.reshape(MLA_Q_LORA, MLA_HEADS, MLA_QK)
    w_uq_h = jnp.pad(w_uq_h, ((0, 0), (0, 0), (0, LANES - MLA_QK))).reshape(MLA_Q_LORA, MLA_HEADS * LANES)
    kgn = jnp.concatenate([mla_k_g[:MLA_NOPE], jnp.ones((LANES - MLA_NOPE,), F32)])[None, :]
    return {
        'ln1': ln1_g[None, :],
        'w_main': w_in[:, :MAIN_WIDTH].astype(BF16),
        'w_kr': jnp.pad(w_in[:, MAIN_WIDTH:], ((0, 0), (0, LANES - MLA_ROPE))).astype(BF16),
        'gq': jnp.tile(moba_q_g, 2)[None, :],
        'gk': jnp.tile(moba_k_g, 2)[None, :],
        'gql': mla_q_lora_g[None, :],
        'w_uq': w_uq_h.astype(BF16),
        'gmq': _head_block(mla_q_g),
        'kgn': kgn,
        'gkv': mla_kv_lora_g[None, :],
        'w_ukT': w_uk.reshape(MLA_KV_LORA, MLA_HEADS * MLA_NOPE).T.astype(BF16),
        'w_uv': w_uv.reshape(MLA_KV_LORA, MLA_HEADS * MLA_V).astype(BF16),
        'kgr': mla_k_g[MLA_NOPE:][:, None],
        'gom': out_g_moba[None, :],
        'gol': out_g_mla[None, :],
        'w_o': w_o.astype(BF16),
        'ln2': ln2_g[None, :],
        'w_up': w_up.astype(BF16),
        'w_down': w_down.astype(BF16),
    }


def _alibi_slopes():
    return jnp.exp2(-8.0 * jnp.arange(1, MOBA_HEADS + 1, dtype=F32) / MOBA_HEADS)


def _prompt_layer(x, w):
    b, t, _ = x.shape
    pos = jnp.arange(t, dtype=jnp.int32)
    x2d = x.reshape(b * t, D_MODEL)
    tm = min(ROW_TILE, b * t)
    assert t % tm == 0
    qm, km, vm, qmla, ckv, kr = _inproj(x2d, _query_rope_tables(pos), t // tm, w)
    o_moba = _moba_prompt(qm.reshape(b, t, -1), km.reshape(b, t, -1), vm.reshape(b, t, -1), _alibi_slopes())
    cos, sin = _rope_angles(pos)
    krT = jnp.transpose(kr.reshape(b, t, -1), (0, 2, 1))
    o_mla = _mla_prompt(qmla.reshape(b, t, -1), ckv.reshape(b, t, -1), krT, w, cos.T, sin.T)
    y = _out_mlp(x2d, o_moba.reshape(b * t, -1), o_mla.reshape(b * t, -1), w)
    return (y.reshape(b, t, D_MODEL), km.reshape(b, t, MOBA_KV_HEADS, HEAD_DIM),
            vm.reshape(b, t, MOBA_KV_HEADS, HEAD_DIM), ckv.reshape(b, t, MLA_KV_LORA), kr.reshape(b, t, MLA_ROPE))


def _pad_rows(a, n):
    return jnp.pad(a, ((0, 0), (0, n - a.shape[1]), (0, 0)))


def _sample_layer(x, w, pool_k, pool_v, pool_c, pool_r, page_table):
    s, t, _ = x.shape
    n_pages = page_table.shape[1]
    page = pool_k.shape[1]
    past_len = n_pages * page
    nq = t * MOBA_HEADS
    x2d = x.reshape(s * t, D_MODEL)
    tm = min(ROW_TILE, s * t)
    assert tm % t == 0
    pos_new = past_len + jnp.arange(t, dtype=jnp.int32)
    qm, km, vm, qmla, ckv, kr = _inproj(x2d, _query_rope_tables(jnp.tile(pos_new, tm // t)), 1, w)

    page_flat = page_table.reshape(-1)
    row = jnp.arange(nq)
    tq_rows = jnp.broadcast_to((row // MOBA_HEADS).astype(F32)[:, None], (nq, LANES))
    slope_rows = jnp.broadcast_to(_alibi_slopes()[row % MOBA_HEADS][:, None], (nq, LANES))

    kv_onehot = jax.nn.one_hot(jnp.arange(MOBA_HEADS) // (MOBA_HEADS // MOBA_KV_HEADS), MOBA_KV_HEADS, dtype=F32)
    qbd = (qm.reshape(s, t, MOBA_HEADS, 1, HEAD_DIM) * kv_onehot[None, None, :, :, None]).reshape(s, nq, MOBA_KV_WIDTH)
    n_new = -(-t // (2 * SUBLANES)) * (2 * SUBLANES)
    page_t = lambda pool: jnp.transpose(pool, (0, 2, 3, 1))
    o_moba = _moba_decode(page_flat, qbd, _pad_rows(km.reshape(s, t, -1), n_new), _pad_rows(vm.reshape(s, t, -1), n_new),
                          slope_rows, tq_rows, page_t(pool_k), page_t(pool_v), n_pages, page)

    qh = qmla.reshape(s, t, MLA_HEADS, LANES)
    head_eye = jnp.eye(MLA_HEADS, dtype=F32)
    qbdn = (qh[..., None, :MLA_NOPE] * head_eye[None, None, :, :, None]).reshape(s, nq, MLA_HEADS * MLA_NOPE)
    qr = qh[..., MLA_NOPE:MLA_QK].reshape(s, nq, MLA_ROPE)
    cos_p, sin_p = _rope_angles(jnp.arange(past_len, dtype=jnp.int32))
    cos_n, sin_n = _rope_angles(past_len + jnp.arange(LANES, dtype=jnp.int32))
    krnew_t = jnp.transpose(_pad_rows(kr.reshape(s, t, -1), LANES), (0, 2, 1))
    o_mla = _mla_decode(page_flat, qbdn, qr, _pad_rows(ckv.reshape(s, t, -1), LANES), krnew_t,
                        tq_rows, w, cos_p.T, sin_p.T, cos_n.T, sin_n.T, pool_c, jnp.transpose(pool_r, (0, 2, 1)),
                        n_pages, page, t)

    y = _out_mlp(x2d, o_moba.reshape(s * t, MOBA_WIDTH), o_mla.reshape(s * t, MLA_WIDTH), w)
    return (y.reshape(s, t, D_MODEL), km.reshape(s, t, MOBA_KV_HEADS, HEAD_DIM),
            vm.reshape(s, t, MOBA_KV_HEADS, HEAD_DIM), ckv.reshape(s, t, MLA_KV_LORA), kr.reshape(s, t, MLA_ROPE))


def kernel(x_prompt, x_sample, cache_moba_k, cache_moba_v, cache_mla_ckv, cache_mla_krope, page_table, ln1_g, w_in,
           moba_q_g, moba_k_g, mla_q_lora_g, w_uq, mla_q_g, mla_kv_lora_g, w_uk, w_uv, mla_k_g, out_g_moba,
           out_g_mla, w_o, ln2_g, w_up, w_down):
    depth = w_in.shape[0]
    layer_weights = (ln1_g, w_in, moba_q_g, moba_k_g, mla_q_lora_g, w_uq, mla_q_g, mla_kv_lora_g, w_uk, w_uv,
                     mla_k_g, out_g_moba, out_g_mla, w_o, ln2_g, w_up, w_down)
    y_p, y_s = x_prompt, x_sample
    outs_p, outs_s = [], []
    for layer in range(depth):
        w = _prep_weights(*(a[layer] for a in layer_weights))
        y_p, *new_p = _prompt_layer(y_p, w)
        y_s, *new_s = _sample_layer(y_s, w, cache_moba_k[layer], cache_moba_v[layer], cache_mla_ckv[layer],
                                    cache_mla_krope[layer], page_table)
        outs_p.append(new_p)
        outs_s.append(new_s)
    stack = lambda outs, k: jnp.stack([o[k] for o in outs])
    return (y_p, y_s, stack(outs_p, 0), stack(outs_p, 1), stack(outs_p, 2), stack(outs_p, 3),
            stack(outs_s, 0), stack(outs_s, 1), stack(outs_s, 2), stack(outs_s, 3))
```

```python
import functools

import jax
import jax.numpy as jnp
from jax import lax
from jax.experimental import pallas as pl
from jax.experimental.pallas import tpu as pltpu

F32 = jnp.float32
BF16 = jnp.bfloat16

D_MODEL = 1024
HEAD_DIM = 64
MOBA_HEADS = 8
MOBA_KV_HEADS = 4
MOBA_WIDTH = MOBA_HEADS * HEAD_DIM
MOBA_KV_WIDTH = MOBA_KV_HEADS * HEAD_DIM
MOBA_BLOCK = 256
MOBA_TOPK = 3
MLA_HEADS = 8
MLA_V = 64
MLA_WIDTH = MLA_HEADS * MLA_V
MLA_NOPE = 64
MLA_ROPE = 32
MLA_HALF = MLA_ROPE // 2
MLA_QK = MLA_NOPE + MLA_ROPE
MLA_Q_LORA = 384
MLA_KV_LORA = 128
ROPE_BASE = 10000.0
D_FF = 4 * D_MODEL
EPS = 1e-6
MAIN_WIDTH = MOBA_WIDTH + 2 * MOBA_KV_WIDTH + MLA_Q_LORA + MLA_KV_LORA

LANES = 128
SUBLANES = 8
VMEM_LIMIT_BYTES = 56 * 1024 * 1024

ROW_TILE = 512
Q_TILE = MOBA_BLOCK
KEY_TILE = 256
MLA_DECODE_SUB_PAGES = 4
NEG = -1e30

_NT = (((1,), (1,)), ((), ()))


def _dot(a, b):
    return jnp.dot(a, b, preferred_element_type=F32)


def _dot_nt(a, b):
    return lax.dot_general(a, b, _NT, preferred_element_type=F32)


def _split_bf16(a):
    hi = a.astype(BF16)
    lo = (a - hi.astype(F32)).astype(BF16)
    return hi, lo


def _dot_nt_f32(a, b):
    ah, al = _split_bf16(a)
    bh, bl = _split_bf16(b)
    return _dot_nt(ah, bh) + _dot_nt(ah, bl) + _dot_nt(al, bh)


def _rms(x, g):
    return x * lax.rsqrt(jnp.mean(x * x, axis=-1, keepdims=True) + EPS) * g


def _const_spec(shape):
    nd = len(shape)
    return pl.BlockSpec(shape, lambda *_: (0,) * nd)


def _inproj_kernel(x_ref, ln1_ref, wmain_ref, wkr_ref, gq_ref, gk_ref, gql_ref, wuq_ref, gmq_ref,
                   kgn_ref, gkv_ref, rc_ref, rs1_ref, rs2_ref,
                   qm_ref, km_ref, vm_ref, qmla_ref, ckv_ref, kr_ref):
    x = x_ref[...]
    hb = _rms(x, ln1_ref[...]).astype(BF16)
    z = _dot(hb, wmain_ref[...])
    zk = _dot(hb, wkr_ref[...])
    tm = x.shape[0]
    lane = lax.broadcasted_iota(jnp.int32, (tm, LANES), 1)
    low = lane < HEAD_DIM

    def pair_norm(blk, g):
        sq = blk * blk
        lo = jnp.sum(jnp.where(low, sq, 0.0), axis=-1, keepdims=True)
        hi = jnp.sum(jnp.where(low, 0.0, sq), axis=-1, keepdims=True)
        inv = jnp.where(low, lax.rsqrt(lo / HEAD_DIM + EPS), lax.rsqrt(hi / HEAD_DIM + EPS))
        return blk * inv * g

    for j in range(MOBA_WIDTH // LANES):
        qm_ref[:, j * LANES:(j + 1) * LANES] = pair_norm(z[:, j * LANES:(j + 1) * LANES], gq_ref[...])
    for j in range(MOBA_KV_WIDTH // LANES):
        o = MOBA_WIDTH + j * LANES
        km_ref[:, j * LANES:(j + 1) * LANES] = pair_norm(z[:, o:o + LANES], gk_ref[...])
    o = MOBA_WIDTH + MOBA_KV_WIDTH
    vm_ref[...] = z[:, o:o + MOBA_KV_WIDTH]
    o += MOBA_KV_WIDTH
    cqn = _rms(z[:, o:o + MLA_Q_LORA], gql_ref[...]).astype(BF16)
    o += MLA_Q_LORA
    ckv_ref[...] = _rms(z[:, o:o + MLA_KV_LORA], gkv_ref[...])
    kr_ref[...] = zk[:, :MLA_ROPE]

    qh = _dot(cqn, wuq_ref[...])
    rc, rs1, rs2 = rc_ref[...], rs1_ref[...], rs2_ref[...]
    gain = gmq_ref[...]
    kgn = kgn_ref[...]
    for h in range(MLA_HEADS):
        blk = qh[:, h * LANES:(h + 1) * LANES]
        ss = jnp.sum(blk * blk, axis=-1, keepdims=True)
        y = blk * lax.rsqrt(ss / MLA_QK + EPS) * gain
        y = y * rc + pltpu.roll(y, MLA_HALF, axis=1) * rs1 + pltpu.roll(y, LANES - MLA_HALF, axis=1) * rs2
        qmla_ref[:, h * LANES:(h + 1) * LANES] = y * kgn


def _inproj(x2d, tabs, n_tab, w):
    rows = x2d.shape[0]
    tm = min(ROW_TILE, rows)
    assert rows % tm == 0 and tm % SUBLANES == 0
    row_spec = lambda width: pl.BlockSpec((tm, width), lambda i: (i, 0))
    tab_spec = pl.BlockSpec((tm, LANES), lambda i: (i % n_tab, 0))
    consts = [w['ln1'], w['w_main'], w['w_kr'], w['gq'], w['gk'], w['gql'], w['w_uq'], w['gmq'], w['kgn'], w['gkv']]
    out_widths = (MOBA_WIDTH, MOBA_KV_WIDTH, MOBA_KV_WIDTH, MLA_HEADS * LANES, MLA_KV_LORA, MLA_ROPE)
    return pl.pallas_call(
        _inproj_kernel,
        grid=(rows // tm,),
        in_specs=[row_spec(D_MODEL)] + [_const_spec(c.shape) for c in consts] + [tab_spec] * 3,
        out_specs=[row_spec(wd) for wd in out_widths],
        out_shape=[jax.ShapeDtypeStruct((rows, wd), F32) for wd in out_widths],
        compiler_params=pltpu.CompilerParams(dimension_semantics=("parallel",), vmem_limit_bytes=VMEM_LIMIT_BYTES),
        name="inproj",
    )(x2d, *consts, *tabs)


def _moba_prompt_kernel(q_ref, k_ref, vt_ref, slope_ref, o_ref, kmean_ref, *, nb):
    i = pl.program_id(1)
    nbp = kmean_ref.shape[0]

    @pl.when(i == 0)
    def _():
        kmean_ref[...] = jnp.zeros_like(kmean_ref)
        for n in range(nb):
            kmean_ref[n:n + 1, :] = jnp.mean(k_ref[0, n * MOBA_BLOCK:(n + 1) * MOBA_BLOCK, :], axis=0, keepdims=True)

    scale = HEAD_DIM ** -0.5
    hpg = MOBA_HEADS // MOBA_KV_HEADS
    nqc = hpg * Q_TILE
    krow = lax.broadcasted_iota(jnp.int32, (KEY_TILE, nqc), 0)
    qcol = lax.broadcasted_iota(jnp.int32, (KEY_TILE, nqc), 1) % Q_TILE
    rel = (qcol - krow).astype(F32)
    blkrow = lax.broadcasted_iota(jnp.int32, (nbp, nqc), 0)
    past = blkrow < i
    second_head = lax.broadcasted_iota(jnp.int32, (1, nqc), 1) >= Q_TILE
    own_start = pl.multiple_of(i * MOBA_BLOCK, MOBA_BLOCK)

    for g in range(MOBA_KV_HEADS):
        kv_cols = slice(g * HEAD_DIM, (g + 1) * HEAD_DIM)
        q2 = jnp.concatenate([q_ref[0, :, (hpg * g + e) * HEAD_DIM:(hpg * g + e + 1) * HEAD_DIM] for e in range(hpg)],
                             axis=0)
        gate = _dot_nt_f32(kmean_ref[:, kv_cols], q2)
        sel = jnp.zeros((nbp, nqc), F32)
        for n in range(nb):
            gn = gate[n:n + 1, :]
            beats = ((gate > gn) | ((gate == gn) & (blkrow < n))) & past
            rank = jnp.sum(beats.astype(F32), axis=0, keepdims=True)
            sel = jnp.where((blkrow == n) & (rank < MOBA_TOPK) & past, 1.0, sel)
        slope_row = jnp.where(second_head, slope_ref[hpg * g + 1], slope_ref[hpg * g])
        bias = slope_row * rel
        qs = (q2 * scale).astype(BF16)

        def past_block(n, carry):
            m, l, acc = carry
            start = pl.multiple_of(n * MOBA_BLOCK, MOBA_BLOCK)
            kb = k_ref[0, pl.ds(start, MOBA_BLOCK), kv_cols].astype(BF16)
            vtb = vt_ref[0, n, kv_cols, :].astype(BF16)
            t_ = _dot_nt(kb, qs) - bias
            shift = slope_row * ((i - n) * MOBA_BLOCK).astype(F32)
            seln = jnp.sum(jnp.where(blkrow == n, sel, 0.0), axis=0, keepdims=True) > 0.5
            m_new = jnp.where(seln, jnp.maximum(m, jnp.max(t_, axis=0, keepdims=True) - shift), m)
            p = jnp.exp(jnp.minimum(t_ - (m_new + shift), 0.0))
            alpha = jnp.exp(m - m_new)
            l = alpha * l + jnp.where(seln, jnp.sum(p, axis=0, keepdims=True), 0.0)
            acc = alpha * acc + jnp.where(seln, _dot(vtb, p.astype(BF16)), 0.0)
            return m_new, l, acc

        init = (jnp.full((1, nqc), NEG, F32), jnp.zeros((1, nqc), F32), jnp.zeros((HEAD_DIM, nqc), F32))
        m, l, acc = lax.fori_loop(0, i, past_block, init)

        kb = k_ref[0, pl.ds(own_start, MOBA_BLOCK), kv_cols].astype(BF16)
        vtb = vt_ref[0, i, kv_cols, :].astype(BF16)
        t_ = jnp.where(rel >= 0.0, _dot_nt(kb, qs) - bias, NEG)
        m_new = jnp.maximum(m, jnp.max(t_, axis=0, keepdims=True))
        p = jnp.exp(t_ - m_new)
        alpha = jnp.exp(m - m_new)
        l = alpha * l + jnp.sum(p, axis=0, keepdims=True)
        out = (alpha * acc + _dot(vtb, p.astype(BF16))) / l
        for e in range(hpg):
            h = hpg * g + e
            o_ref[0, h * HEAD_DIM:(h + 1) * HEAD_DIM, :] = out[:, e * Q_TILE:(e + 1) * Q_TILE]


def _moba_prompt(qm, km, vm, slopes):
    b, t, _ = qm.shape
    assert t % MOBA_BLOCK == 0
    nb = t // MOBA_BLOCK
    assert nb >= MOBA_TOPK
    nbp = -(-nb // SUBLANES) * SUBLANES
    vt = jnp.transpose(vm.reshape(b, nb, MOBA_BLOCK, MOBA_KV_WIDTH), (0, 1, 3, 2))
    return pl.pallas_call(
        functools.partial(_moba_prompt_kernel, nb=nb),
        grid=(b, nb),
        in_specs=[pl.BlockSpec((1, Q_TILE, MOBA_WIDTH), lambda bi, i: (bi, i, 0)),
                  pl.BlockSpec((1, t, MOBA_KV_WIDTH), lambda bi, i: (bi, 0, 0)),
                  pl.BlockSpec((1, nb, MOBA_KV_WIDTH, MOBA_BLOCK), lambda bi, i: (bi, 0, 0, 0)),
                  pl.BlockSpec(memory_space=pltpu.SMEM)],
        out_specs=pl.BlockSpec((1, MOBA_WIDTH, Q_TILE), lambda bi, i: (bi, 0, i)),
        out_shape=jax.ShapeDtypeStruct((b, MOBA_WIDTH, t), F32),
        scratch_shapes=[pltpu.VMEM((nbp, MOBA_KV_WIDTH), F32)],
        compiler_params=pltpu.CompilerParams(dimension_semantics=("parallel", "arbitrary"),
                                             vmem_limit_bytes=VMEM_LIMIT_BYTES),
        name="moba_prompt",
    )(qm, km, vt, slopes)


def _mla_key_tile(c_bf, krT, wukT, kgr, cos, sin):
    kT = _dot_nt(wukT, c_bf)
    rot, inv = _mla_key_norm_rope(kT, krT, kgr, cos, sin)
    return kT, rot, inv


def _mla_key_norm_rope(kT, krT, kgr, cos, sin):
    n_keys = kT.shape[1]
    kn = kT[:MLA_HEADS * MLA_NOPE]
    ssq = jnp.sum((kn * kn).reshape(MLA_HEADS, MLA_NOPE, n_keys), axis=1)
    krsq = jnp.sum(krT * krT, axis=0, keepdims=True)
    inv = lax.rsqrt((ssq + krsq) / MLA_QK + EPS) * (MLA_QK ** -0.5)
    x = krT * kgr
    x1, x2 = x[:MLA_HALF], x[MLA_HALF:]
    rot = jnp.concatenate([x1 * cos - x2 * sin, x1 * sin + x2 * cos], axis=0)
    return rot, inv


def _mla_prompt_kernel(q_ref, c_ref, krT_ref, wkvT_ref, kgr_ref, cos_ref, sin_ref,
                       o_ref, khat_ref, vhat_ref, *, nb):
    i = pl.program_id(1)
    n_hd = MLA_HEADS * MLA_NOPE

    @pl.when(i == 0)
    def _():
        for n in range(nb):
            rows = slice(n * KEY_TILE, (n + 1) * KEY_TILE)
            cb = c_ref[0, rows, :].astype(BF16)
            kvT, rot, inv = _mla_key_tile(cb, krT_ref[0, :, rows], wkvT_ref[...], kgr_ref[...],
                                          cos_ref[:, rows], sin_ref[:, rows])
            pad = jnp.zeros((LANES - MLA_QK, KEY_TILE), F32)
            for h in range(MLA_HEADS):
                sc = inv[h:h + 1]
                kh = jnp.concatenate([kvT[h * MLA_NOPE:(h + 1) * MLA_NOPE] * sc, rot * sc, pad], axis=0)
                khat_ref[h, n] = kh.T.astype(BF16)
                vhat_ref[h, n] = kvT[n_hd + h * MLA_V:n_hd + (h + 1) * MLA_V].astype(BF16)

    krow = lax.broadcasted_iota(jnp.int32, (KEY_TILE, Q_TILE), 0)
    qcol = lax.broadcasted_iota(jnp.int32, (KEY_TILE, Q_TILE), 1)
    causal = krow <= qcol
    heads_per_iter = 2

    def step(h, n, qb, carry, mask):
        m, l, acc = carry
        s = _dot_nt(khat_ref[h, n], qb)
        if mask is not None:
            s = jnp.where(mask, s, NEG)
        m_new = jnp.maximum(m, jnp.max(s, axis=0, keepdims=True))
        alpha = jnp.exp(m - m_new)
        p = jnp.exp(s - m_new)
        l = alpha * l + jnp.sum(p, axis=0, keepdims=True)
        acc = alpha * acc + _dot(vhat_ref[h, n], p.astype(BF16))
        return m_new, l, acc

    for h0 in range(0, MLA_HEADS, heads_per_iter):
        heads = range(h0, h0 + heads_per_iter)
        qbs = [q_ref[0, :, h * LANES:(h + 1) * LANES].astype(BF16) for h in heads]

        def past_block(n, carries):
            return tuple(step(h, n, qb, c, None) for h, qb, c in zip(heads, qbs, carries))

        init = (jnp.full((1, Q_TILE), NEG, F32), jnp.zeros((1, Q_TILE), F32), jnp.zeros((MLA_V, Q_TILE), F32))
        carries = lax.fori_loop(0, i, past_block, (init,) * heads_per_iter)
        for h, qb, c in zip(heads, qbs, carries):
            _, l, acc = step(h, i, qb, c, causal)
            o_ref[0, h * MLA_V:(h + 1) * MLA_V, :] = acc / l


def _mla_prompt(qmla, ckv, krT, w, cos_t, sin_t):
    b, t, _ = qmla.shape
    assert t % KEY_TILE == 0
    nb = t // KEY_TILE
    consts = [w['w_kvT'], w['kgr'], cos_t, sin_t]
    return pl.pallas_call(
        functools.partial(_mla_prompt_kernel, nb=nb),
        grid=(b, nb),
        in_specs=[pl.BlockSpec((1, Q_TILE, MLA_HEADS * LANES), lambda bi, i: (bi, i, 0)),
                  pl.BlockSpec((1, t, MLA_KV_LORA), lambda bi, i: (bi, 0, 0)),
                  pl.BlockSpec((1, MLA_ROPE, t), lambda bi, i: (bi, 0, 0))]
                 + [_const_spec(c.shape) for c in consts],
        out_specs=pl.BlockSpec((1, MLA_WIDTH, Q_TILE), lambda bi, i: (bi, 0, i)),
        out_shape=jax.ShapeDtypeStruct((b, MLA_WIDTH, t), F32),
        scratch_shapes=[pltpu.VMEM((MLA_HEADS, nb, KEY_TILE, LANES), BF16),
                        pltpu.VMEM((MLA_HEADS, nb, MLA_V, KEY_TILE), BF16)],
        compiler_params=pltpu.CompilerParams(dimension_semantics=("parallel", "arbitrary"),
                                             vmem_limit_bytes=VMEM_LIMIT_BYTES),
        name="mla_prompt",
    )(qmla, ckv, krT, *consts)


def _moba_decode_kernel(pt_ref, qbd_ref, knew_ref, vnew_ref, slope_ref, tq_ref, *rest, pps, past_len):
    del pt_ref
    kpages, vpages = rest[:pps], rest[pps:2 * pps]
    o_ref = rest[2 * pps]
    gate_s, m_s, l_s, o_s = rest[2 * pps + 1:]
    j = pl.program_id(1)
    nblk = gate_s.shape[0]
    nq = qbd_ref.shape[1]
    scale = HEAD_DIM ** -0.5
    bps = pps // 2

    qbd = qbd_ref[0]
    qb = qbd.astype(BF16)
    slope = slope_ref[...]
    tq = tq_ref[...]
    slope2 = jnp.concatenate([slope, slope], axis=1)
    tq2 = jnp.concatenate([tq, tq], axis=1)
    lane2 = lax.broadcasted_iota(jnp.int32, (nq, MOBA_BLOCK), 1).astype(F32)

    def page_t(ref):
        return ref[0].reshape(MOBA_KV_WIDTH, ref.shape[3]).astype(BF16)

    for bb in range(bps):
        ktb = jnp.concatenate([page_t(kpages[2 * bb]), page_t(kpages[2 * bb + 1])], axis=1)
        vtb = jnp.concatenate([page_t(vpages[2 * bb]), page_t(vpages[2 * bb + 1])], axis=1)
        blk = j * bps + bb
        dist = (past_len + tq2) - ((blk * MOBA_BLOCK).astype(F32) + lane2)
        qk = _dot(qb, ktb)
        gate = jnp.sum(qk, axis=-1, keepdims=True) / MOBA_BLOCK
        s = qk * scale - slope2 * dist
        m = jnp.max(s, axis=-1, keepdims=True)
        p = jnp.exp(s - m)
        l = jnp.sum(p, axis=-1, keepdims=True)
        gate_s[blk] = jnp.broadcast_to(gate, (nq, LANES))
        m_s[blk] = jnp.broadcast_to(m, (nq, LANES))
        l_s[blk] = jnp.broadcast_to(l, (nq, LANES))
        o_s[blk] = _dot_nt(p.astype(BF16), vtb)

    @pl.when(j == pl.num_programs(1) - 1)
    def _():
        gates = gate_s[...]
        bidx = lax.broadcasted_iota(jnp.int32, gates.shape, 0)
        picked = jnp.zeros(gates.shape, F32)
        for _ in range(min(MOBA_TOPK, nblk)):
            mx = jnp.max(gates, axis=0, keepdims=True)
            first = jnp.min(jnp.where(gates == mx, bidx, nblk), axis=0, keepdims=True)
            hit = bidx == first
            picked = jnp.where(hit, 1.0, picked)
            gates = jnp.where(hit, -jnp.inf, gates)
        sel = picked > 0.5

        n_new = knew_ref.shape[1]
        tk = lax.broadcasted_iota(jnp.int32, (nq, n_new), 1).astype(F32)
        tq_n = tq[:, :n_new]
        dist_o = tq_n - tk
        s_o = _dot_nt(qb, knew_ref[0].astype(BF16)) * scale - slope[:, :n_new] * dist_o
        mask_o = dist_o >= 0.0
        s_o = jnp.where(mask_o, s_o, NEG)
        m_o = jnp.broadcast_to(jnp.max(s_o, axis=-1, keepdims=True), (nq, LANES))

        m_all = m_s[...]
        m_fin = jnp.maximum(jnp.max(jnp.where(sel, m_all, NEG), axis=0), m_o)
        wgt = jnp.exp(jnp.where(sel, m_all - m_fin[None], -jnp.inf))
        p_o = jnp.where(mask_o, jnp.exp(s_o - m_fin[:, :n_new]), 0.0)
        l_fin = jnp.sum(wgt * l_s[...], axis=0) + jnp.sum(p_o, axis=-1, keepdims=True)
        wgt2 = jnp.concatenate([wgt, wgt], axis=2)
        o_fin = jnp.sum(wgt2 * o_s[...], axis=0) + _dot(p_o.astype(BF16), vnew_ref[0].astype(BF16))
        o_fin = o_fin / jnp.concatenate([l_fin, l_fin], axis=1)

        row = lax.broadcasted_iota(jnp.int32, (nq, HEAD_DIM), 0)
        grp = (row % MOBA_HEADS) // (MOBA_HEADS // MOBA_KV_HEADS)
        out = jnp.zeros((nq, HEAD_DIM), F32)
        for g in range(MOBA_KV_HEADS):
            out = jnp.where(grp == g, o_fin[:, g * HEAD_DIM:(g + 1) * HEAD_DIM], out)
        o_ref[0] = out


def _pages_per_step(n_pages):
    pps = min(16, n_pages)
    assert n_pages % pps == 0 and pps % 2 == 0
    return pps


def _moba_decode(page_flat, qbd, knew, vnew, slope_rows, tq_rows, pool_kt, pool_vt, n_pages, page):
    s, nq, _ = qbd.shape
    assert 2 * page == MOBA_BLOCK and pool_kt.shape[1:] == (MOBA_KV_HEADS, HEAD_DIM, page)
    pps = _pages_per_step(n_pages)
    nblk = n_pages // 2
    n_new = knew.shape[1]

    def page_spec(kk):
        return pl.BlockSpec((1, MOBA_KV_HEADS, HEAD_DIM, page),
                            lambda si, j, pt: (pt[si * n_pages + j * pps + kk], 0, 0, 0))

    per_seq = lambda shape: pl.BlockSpec((1,) + shape, lambda si, j, pt: (si, 0, 0))
    const2 = pl.BlockSpec((nq, LANES), lambda si, j, pt: (0, 0))
    grid_spec = pltpu.PrefetchScalarGridSpec(
        num_scalar_prefetch=1,
        grid=(s, n_pages // pps),
        in_specs=[per_seq((nq, MOBA_KV_WIDTH)), per_seq((n_new, MOBA_KV_WIDTH)), per_seq((n_new, MOBA_KV_WIDTH)),
                  const2, const2]
                 + [page_spec(kk) for kk in range(pps)] * 2,
        out_specs=per_seq((nq, HEAD_DIM)),
        scratch_shapes=[pltpu.VMEM((nblk, nq, LANES), F32)] * 3 + [pltpu.VMEM((nblk, nq, MOBA_KV_WIDTH), F32)],
    )
    return pl.pallas_call(
        functools.partial(_moba_decode_kernel, pps=pps, past_len=float(n_pages * page)),
        grid_spec=grid_spec,
        out_shape=jax.ShapeDtypeStruct((s, nq, HEAD_DIM), F32),
        compiler_params=pltpu.CompilerParams(dimension_semantics=("parallel", "arbitrary"),
                                             vmem_limit_bytes=VMEM_LIMIT_BYTES),
        name="moba_decode",
    )(page_flat, qbd, knew, vnew, slope_rows, tq_rows, *([pool_kt] * pps), *([pool_vt] * pps))


def _mla_decode_kernel(pt_ref, qbdn_ref, qr_ref, cnew_ref, krnew_ref, tq_ref, wukT_ref, wuv_ref, kgr_ref,
                       cos_ref, sin_ref, cosn_ref, sinn_ref, *rest, pps, sub, n_tok):
    del pt_ref
    cpages, rpages = rest[:pps], rest[pps:2 * pps]
    o_ref = rest[2 * pps]
    lhs_s, m_s, l_s, acc_s = rest[2 * pps + 1:]
    j = pl.program_id(1)
    nq = qr_ref.shape[1]
    page = cpages[0].shape[1]
    n_hd = MLA_HEADS * MLA_NOPE

    @pl.when(j == 0)
    def _():
        lhs_s[:n_hd, :] = wukT_ref[...]
        lhs_s[n_hd:, :] = _dot(qbdn_ref[0].astype(BF16), wukT_ref[...]).astype(BF16)

    qr = qr_ref[0].astype(BF16)

    def partial_softmax(cb, krT, cos, sin, mask, kT=None):
        if kT is None:
            kT = _dot_nt(lhs_s[...], cb)
        rot, inv = _mla_key_norm_rope(kT, krT, kgr_ref[...], cos, sin)
        inv_rows = jnp.concatenate([inv] * (nq // MLA_HEADS), axis=0)
        s = (kT[n_hd:] + _dot(qr, rot.astype(BF16))) * inv_rows
        if mask is not None:
            s = jnp.where(mask, s, NEG)
        m = jnp.max(s, axis=-1, keepdims=True)
        p = jnp.exp(s - m)
        if mask is not None:
            p = jnp.where(mask, p, 0.0)
        return m, jnp.sum(p, axis=-1, keepdims=True), _dot(p.astype(BF16), cb)

    n_sub = pps // sub
    latent = lambda sc: jnp.concatenate([cpages[k][0] for k in range(sc * sub, (sc + 1) * sub)], axis=0).astype(BF16)
    cb_next = latent(0)
    kT_next = _dot_nt(lhs_s[...], cb_next)
    for sc in range(n_sub):
        cb, kT = cb_next, kT_next
        if sc + 1 < n_sub:
            cb_next = latent(sc + 1)
            kT_next = _dot_nt(lhs_s[...], cb_next)
        krT = jnp.concatenate([rpages[k][0] for k in range(sc * sub, (sc + 1) * sub)], axis=1)
        cols = slice(sc * sub * page, (sc + 1) * sub * page)
        m, l, acc = partial_softmax(cb, krT, cos_ref[:, cols], sin_ref[:, cols], None, kT)
        slot = j * (pps // sub) + sc
        m_s[slot] = jnp.broadcast_to(m, (nq, LANES))
        l_s[slot] = jnp.broadcast_to(l, (nq, LANES))
        acc_s[slot] = acc

    @pl.when(j == pl.num_programs(1) - 1)
    def _():
        tk = lax.broadcasted_iota(jnp.int32, (nq, LANES), 1).astype(F32)
        mask = (tk <= tq_ref[...]) & (tk < n_tok)
        m_n, l_n, acc_n = partial_softmax(cnew_ref[0].astype(BF16), krnew_ref[0], cosn_ref[...], sinn_ref[...], mask)
        m_all = m_s[...]
        m_fin = jnp.maximum(jnp.max(m_all, axis=0), m_n)
        wgt = jnp.exp(m_all - m_fin[None])
        w_n = jnp.exp(m_n - m_fin)
        l_fin = jnp.sum(wgt * l_s[...], axis=0) + w_n * l_n
        a_fin = jnp.sum(wgt * acc_s[...], axis=0) + w_n * acc_n
        o_lat = (a_fin / l_fin).astype(BF16)
        full = _dot(o_lat, wuv_ref[...])
        row = lax.broadcasted_iota(jnp.int32, (nq, MLA_V), 0)
        head = row % MLA_HEADS
        out = jnp.zeros((nq, MLA_V), F32)
        for h in range(MLA_HEADS):
            out = jnp.where(head == h, full[:, h * MLA_V:(h + 1) * MLA_V], out)
        o_ref[0] = out


def _mla_decode(page_flat, qbdn, qr, cnew, krnew_t, tq_rows, w, cos_p, sin_p, cos_n, sin_n,
                pool_c, pool_rt, n_pages, page, n_tok):
    s, nq, _ = qr.shape
    pps = _pages_per_step(n_pages)
    sub = min(MLA_DECODE_SUB_PAGES, pps)
    assert page == LANES and pps % sub == 0
    n_slots = n_pages // sub

    def c_spec(kk):
        return pl.BlockSpec((1, page, MLA_KV_LORA), lambda si, j, pt: (pt[si * n_pages + j * pps + kk], 0, 0))

    def r_spec(kk):
        return pl.BlockSpec((1, MLA_ROPE, page), lambda si, j, pt: (pt[si * n_pages + j * pps + kk], 0, 0))

    per_seq = lambda shape: pl.BlockSpec((1,) + shape, lambda si, j, pt: (si, 0, 0))
    cst = lambda a: pl.BlockSpec(a.shape, lambda si, j, pt: (0,) * a.ndim)
    tab = pl.BlockSpec((MLA_HALF, pps * page), lambda si, j, pt: (0, j))
    n_lhs = MLA_HEADS * MLA_NOPE + nq
    grid_spec = pltpu.PrefetchScalarGridSpec(
        num_scalar_prefetch=1,
        grid=(s, n_pages // pps),
        in_specs=[per_seq((nq, MLA_HEADS * MLA_NOPE)), per_seq((nq, MLA_ROPE)),
                  per_seq((LANES, MLA_KV_LORA)), per_seq((MLA_ROPE, LANES)), cst(tq_rows),
                  cst(w['w_ukT']), cst(w['w_uv']), cst(w['kgr']), tab, tab, cst(cos_n), cst(sin_n)]
                 + [c_spec(kk) for kk in range(pps)] + [r_spec(kk) for kk in range(pps)],
        out_specs=per_seq((nq, MLA_V)),
        scratch_shapes=[pltpu.VMEM((n_lhs, MLA_KV_LORA), BF16), pltpu.VMEM((n_slots, nq, LANES), F32),
                        pltpu.VMEM((n_slots, nq, LANES), F32), pltpu.VMEM((n_slots, nq, MLA_KV_LORA), F32)],
    )
    return pl.pallas_call(
        functools.partial(_mla_decode_kernel, pps=pps, sub=sub, n_tok=float(n_tok)),
        grid_spec=grid_spec,
        out_shape=jax.ShapeDtypeStruct((s, nq, MLA_V), F32),
        compiler_params=pltpu.CompilerParams(dimension_semantics=("parallel", "arbitrary"),
                                             vmem_limit_bytes=VMEM_LIMIT_BYTES),
        name="mla_decode",
    )(page_flat, qbdn, qr, cnew, krnew_t, tq_rows, w['w_ukT'], w['w_uv'], w['kgr'], cos_p, sin_p, cos_n, sin_n,
      *([pool_c] * pps), *([pool_rt] * pps))


def _out_mlp_kernel(x_ref, om_ref, ol_ref, gom_ref, gol_ref, wo_ref, ln2_ref, wup_ref, wdn_ref, y_ref):
    mixed = jnp.concatenate([_rms(om_ref[...], gom_ref[...]), _rms(ol_ref[...], gol_ref[...])], axis=-1)
    x2 = x_ref[...] + _dot(mixed.astype(BF16), wo_ref[...])
    hb = _rms(x2, ln2_ref[...]).astype(BF16)
    acc = x2
    for c in range(D_FF // D_MODEL):
        cols = slice(c * D_MODEL, (c + 1) * D_MODEL)
        u = jnp.maximum(_dot(hb, wup_ref[:, cols]), 0.0)
        acc = acc + _dot((u * u).astype(BF16), wdn_ref[cols, :])
    y_ref[...] = acc


def _out_mlp(x2d, om, ol, w):
    rows = x2d.shape[0]
    tm = min(ROW_TILE, rows)
    assert rows % tm == 0
    row_spec = lambda width: pl.BlockSpec((tm, width), lambda i: (i, 0))
    single = lambda a: pl.BlockSpec(a.shape, lambda i: (0,) * a.ndim, pipeline_mode=pl.Buffered(1))
    consts = [w['gom'], w['gol'], w['w_o'], w['ln2'], w['w_up'], w['w_down']]
    return pl.pallas_call(
        _out_mlp_kernel,
        grid=(rows // tm,),
        in_specs=[row_spec(D_MODEL), row_spec(MOBA_WIDTH), row_spec(MLA_WIDTH)] + [single(c) for c in consts],
        out_specs=row_spec(D_MODEL),
        out_shape=jax.ShapeDtypeStruct((rows, D_MODEL), F32),
        compiler_params=pltpu.CompilerParams(dimension_semantics=("parallel",), vmem_limit_bytes=VMEM_LIMIT_BYTES),
        name="out_mlp",
    )(x2d, om, ol, *consts)


def _rope_angles(pos):
    inv_freq = ROPE_BASE ** (-jnp.arange(MLA_HALF, dtype=F32) / MLA_HALF)
    ang = pos.astype(F32)[:, None] * inv_freq[None, :]
    return jnp.cos(ang), jnp.sin(ang)


def _query_rope_tables(pos):
    cos, sin = _rope_angles(pos)
    n = pos.shape[0]
    rc = jnp.concatenate([jnp.ones((n, MLA_NOPE), F32), cos, cos, jnp.zeros((n, LANES - MLA_QK), F32)], axis=1)
    zeros = lambda k: jnp.zeros((n, k), F32)
    rs1 = jnp.concatenate([zeros(MLA_NOPE + MLA_HALF), sin, zeros(LANES - MLA_QK)], axis=1)
    rs2 = jnp.concatenate([zeros(MLA_NOPE), -sin, zeros(LANES - MLA_NOPE - MLA_HALF)], axis=1)
    return rc, rs1, rs2


def _head_block(v):
    return jnp.concatenate([v, jnp.zeros((LANES - MLA_QK,), F32)])[None, :]


def _prep_weights(ln1_g, w_in, moba_q_g, moba_k_g, mla_q_lora_g, w_uq, mla_q_g, mla_kv_lora_g, w_uk, w_uv,
                  mla_k_g, out_g_moba, out_g_mla, w_o, ln2_g, w_up, w_down):
    w_uq_h = w_uq.reshape(MLA_Q_LORA, MLA_HEADS, MLA_QK)
    w_uq_h = jnp.pad(w_uq_h, ((0, 0), (0, 0), (0, LANES - MLA_QK))).reshape(MLA_Q_LORA, MLA_HEADS * LANES)
    kgn = jnp.concatenate([mla_k_g[:MLA_NOPE], jnp.ones((LANES - MLA_NOPE,), F32)])[None, :]
    return {
        'ln1': ln1_g[None, :],
        'w_main': w_in[:, :MAIN_WIDTH].astype(BF16),
        'w_kr': jnp.pad(w_in[:, MAIN_WIDTH:], ((0, 0), (0, LANES - MLA_ROPE))).astype(BF16),
        'gq': jnp.tile(moba_q_g, 2)[None, :],
        'gk': jnp.tile(moba_k_g, 2)[None, :],
        'gql': mla_q_lora_g[None, :],
        'w_uq': w_uq_h.astype(BF16),
        'gmq': _head_block(mla_q_g),
        'kgn': kgn,
        'gkv': mla_kv_lora_g[None, :],
        'w_ukT': w_uk.reshape(MLA_KV_LORA, MLA_HEADS * MLA_NOPE).T.astype(BF16),
        'w_uv': w_uv.reshape(MLA_KV_LORA, MLA_HEADS * MLA_V).astype(BF16),
        'w_kvT': jnp.concatenate([w_uk.reshape(MLA_KV_LORA, MLA_HEADS * MLA_NOPE).T,
                                  w_uv.reshape(MLA_KV_LORA, MLA_HEADS * MLA_V).T]).astype(BF16),
        'kgr': mla_k_g[MLA_NOPE:][:, None],
        'gom': out_g_moba[None, :],
        'gol': out_g_mla[None, :],
        'w_o': w_o.astype(BF16),
        'ln2': ln2_g[None, :],
        'w_up': w_up.astype(BF16),
        'w_down': w_down.astype(BF16),
    }


def _alibi_slopes():
    return jnp.exp2(-8.0 * jnp.arange(1, MOBA_HEADS + 1, dtype=F32) / MOBA_HEADS)


def _prompt_layer(x, w):
    b, t, _ = x.shape
    pos = jnp.arange(t, dtype=jnp.int32)
    x2d = x.reshape(b * t, D_MODEL)
    tm = min(ROW_TILE, b * t)
    assert t % tm == 0
    qm, km, vm, qmla, ckv, kr = _inproj(x2d, _query_rope_tables(pos), t // tm, w)
    o_moba_t = _moba_prompt(qm.reshape(b, t, -1), km.reshape(b, t, -1), vm.reshape(b, t, -1), _alibi_slopes())
    o_moba = jnp.transpose(o_moba_t, (0, 2, 1))
    cos, sin = _rope_angles(pos)
    krT = jnp.transpose(kr.reshape(b, t, -1), (0, 2, 1))
    o_mla_t = _mla_prompt(qmla.reshape(b, t, -1), ckv.reshape(b, t, -1), krT, w, cos.T, sin.T)
    o_mla = jnp.transpose(o_mla_t, (0, 2, 1))
    y = _out_mlp(x2d, o_moba.reshape(b * t, -1), o_mla.reshape(b * t, -1), w)
    return (y.reshape(b, t, D_MODEL), km.reshape(b, t, MOBA_KV_HEADS, HEAD_DIM),
            vm.reshape(b, t, MOBA_KV_HEADS, HEAD_DIM), ckv.reshape(b, t, MLA_KV_LORA), kr.reshape(b, t, MLA_ROPE))


def _pad_rows(a, n):
    return jnp.pad(a, ((0, 0), (0, n - a.shape[1]), (0, 0)))


def _sample_layer(x, w, pool_k, pool_v, pool_c, pool_r, page_table):
    s, t, _ = x.shape
    n_pages = page_table.shape[1]
    page = pool_k.shape[1]
    past_len = n_pages * page
    nq = t * MOBA_HEADS
    x2d = x.reshape(s * t, D_MODEL)
    tm = min(ROW_TILE, s * t)
    assert tm % t == 0
    pos_new = past_len + jnp.arange(t, dtype=jnp.int32)
    qm, km, vm, qmla, ckv, kr = _inproj(x2d, _query_rope_tables(jnp.tile(pos_new, tm // t)), 1, w)

    page_flat = page_table.reshape(-1)
    row = jnp.arange(nq)
    tq_rows = jnp.broadcast_to((row // MOBA_HEADS).astype(F32)[:, None], (nq, LANES))
    slope_rows = jnp.broadcast_to(_alibi_slopes()[row % MOBA_HEADS][:, None], (nq, LANES))

    kv_onehot = jax.nn.one_hot(jnp.arange(MOBA_HEADS) // (MOBA_HEADS // MOBA_KV_HEADS), MOBA_KV_HEADS, dtype=F32)
    qbd = (qm.reshape(s, t, MOBA_HEADS, 1, HEAD_DIM) * kv_onehot[None, None, :, :, None]).reshape(s, nq, MOBA_KV_WIDTH)
    n_new = -(-t // (2 * SUBLANES)) * (2 * SUBLANES)
    page_t = lambda pool: jnp.transpose(pool, (0, 2, 3, 1))
    o_moba = _moba_decode(page_flat, qbd, _pad_rows(km.reshape(s, t, -1), n_new), _pad_rows(vm.reshape(s, t, -1), n_new),
                          slope_rows, tq_rows, page_t(pool_k), page_t(pool_v), n_pages, page)

    qh = qmla.reshape(s, t, MLA_HEADS, LANES)
    head_eye = jnp.eye(MLA_HEADS, dtype=F32)
    qbdn = (qh[..., None, :MLA_NOPE] * head_eye[None, None, :, :, None]).reshape(s, nq, MLA_HEADS * MLA_NOPE)
    qr = qh[..., MLA_NOPE:MLA_QK].reshape(s, nq, MLA_ROPE)
    cos_p, sin_p = _rope_angles(jnp.arange(past_len, dtype=jnp.int32))
    cos_n, sin_n = _rope_angles(past_len + jnp.arange(LANES, dtype=jnp.int32))
    krnew_t = jnp.transpose(_pad_rows(kr.reshape(s, t, -1), LANES), (0, 2, 1))
    o_mla = _mla_decode(page_flat, qbdn, qr, _pad_rows(ckv.reshape(s, t, -1), LANES), krnew_t,
                        tq_rows, w, cos_p.T, sin_p.T, cos_n.T, sin_n.T, pool_c, jnp.transpose(pool_r, (0, 2, 1)),
                        n_pages, page, t)

    y = _out_mlp(x2d, o_moba.reshape(s * t, MOBA_WIDTH), o_mla.reshape(s * t, MLA_WIDTH), w)
    return (y.reshape(s, t, D_MODEL), km.reshape(s, t, MOBA_KV_HEADS, HEAD_DIM),
            vm.reshape(s, t, MOBA_KV_HEADS, HEAD_DIM), ckv.reshape(s, t, MLA_KV_LORA), kr.reshape(s, t, MLA_ROPE))


def kernel(x_prompt, x_sample, cache_moba_k, cache_moba_v, cache_mla_ckv, cache_mla_krope, page_table, ln1_g, w_in,
           moba_q_g, moba_k_g, mla_q_lora_g, w_uq, mla_q_g, mla_kv_lora_g, w_uk, w_uv, mla_k_g, out_g_moba,
           out_g_mla, w_o, ln2_g, w_up, w_down):
    depth = w_in.shape[0]
    layer_weights = (ln1_g, w_in, moba_q_g, moba_k_g, mla_q_lora_g, w_uq, mla_q_g, mla_kv_lora_g, w_uk, w_uv,
                     mla_k_g, out_g_moba, out_g_mla, w_o, ln2_g, w_up, w_down)
    y_p, y_s = x_prompt, x_sample
    outs_p, outs_s = [], []
    for layer in range(depth):
        w = _prep_weights(*(a[layer] for a in layer_weights))
        y_p, *new_p = _prompt_layer(y_p, w)
        y_s, *new_s = _sample_layer(y_s, w, cache_moba_k[layer], cache_moba_v[layer], cache_mla_ckv[layer],
                                    cache_mla_krope[layer], page_table)
        outs_p.append(new_p)
        outs_s.append(new_s)
    stack = lambda outs, k: jnp.stack([o[k] for o in outs])
    return (y_p, y_s, stack(outs_p, 0), stack(outs_p, 1), stack(outs_p, 2), stack(outs_p, 3),
            stack(outs_s, 0), stack(outs_s, 1), stack(outs_s, 2), stack(outs_s, 3))
```

```python
import functools

import jax
import jax.numpy as jnp
from jax import lax
from jax.experimental import pallas as pl
from jax.experimental.pallas import tpu as pltpu

F32 = jnp.float32
BF16 = jnp.bfloat16

D_MODEL = 1024
HEAD_DIM = 64
MOBA_HEADS = 8
MOBA_KV_HEADS = 4
MOBA_WIDTH = MOBA_HEADS * HEAD_DIM
MOBA_KV_WIDTH = MOBA_KV_HEADS * HEAD_DIM
MOBA_BLOCK = 256
MOBA_TOPK = 3
MLA_HEADS = 8
MLA_V = 64
MLA_WIDTH = MLA_HEADS * MLA_V
MLA_NOPE = 64
MLA_ROPE = 32
MLA_HALF = MLA_ROPE // 2
MLA_QK = MLA_NOPE + MLA_ROPE
MLA_Q_LORA = 384
MLA_KV_LORA = 128
ROPE_BASE = 10000.0
D_FF = 4 * D_MODEL
EPS = 1e-6
MAIN_WIDTH = MOBA_WIDTH + 2 * MOBA_KV_WIDTH + MLA_Q_LORA + MLA_KV_LORA

LANES = 128
SUBLANES = 8
VMEM_LIMIT_BYTES = 56 * 1024 * 1024

ROW_TILE = 512
Q_TILE = MOBA_BLOCK
KEY_TILE = 256
MOBA_DECODE_PAGES = 16
MLA_DECODE_PAGES = 32
MLA_DECODE_SUB_PAGES = 32
NEG = -1e30

_NT = (((1,), (1,)), ((), ()))


def _dot(a, b):
    return jnp.dot(a, b, preferred_element_type=F32)


def _dot_nt(a, b):
    return lax.dot_general(a, b, _NT, preferred_element_type=F32)


def _split_bf16(a):
    hi = a.astype(BF16)
    lo = (a - hi.astype(F32)).astype(BF16)
    return hi, lo


def _dot_nt_f32(a, b):
    ah, al = _split_bf16(a)
    bh, bl = _split_bf16(b)
    return _dot_nt(ah, bh) + _dot_nt(ah, bl) + _dot_nt(al, bh)


def _rms(x, g):
    return x * lax.rsqrt(jnp.mean(x * x, axis=-1, keepdims=True) + EPS) * g


def _const_spec(shape):
    nd = len(shape)
    return pl.BlockSpec(shape, lambda *_: (0,) * nd)


def _inproj_kernel(x_ref, ln1_ref, wmain_ref, wkr_ref, gq_ref, gk_ref, gql_ref, wuq_ref, gmq_ref,
                   kgn_ref, gkv_ref, rc_ref, rs1_ref, rs2_ref,
                   qm_ref, km_ref, vm_ref, qmla_ref, ckv_ref, kr_ref):
    x = x_ref[...]
    hb = _rms(x, ln1_ref[...]).astype(BF16)
    z = _dot(hb, wmain_ref[...])
    zk = _dot(hb, wkr_ref[...])
    tm = x.shape[0]
    lane = lax.broadcasted_iota(jnp.int32, (tm, LANES), 1)
    low = lane < HEAD_DIM

    def pair_norm(blk, g):
        sq = blk * blk
        lo = jnp.sum(jnp.where(low, sq, 0.0), axis=-1, keepdims=True)
        hi = jnp.sum(jnp.where(low, 0.0, sq), axis=-1, keepdims=True)
        inv = jnp.where(low, lax.rsqrt(lo / HEAD_DIM + EPS), lax.rsqrt(hi / HEAD_DIM + EPS))
        return blk * inv * g

    for j in range(MOBA_WIDTH // LANES):
        qm_ref[:, j * LANES:(j + 1) * LANES] = pair_norm(z[:, j * LANES:(j + 1) * LANES], gq_ref[...])
    for j in range(MOBA_KV_WIDTH // LANES):
        o = MOBA_WIDTH + j * LANES
        km_ref[:, j * LANES:(j + 1) * LANES] = pair_norm(z[:, o:o + LANES], gk_ref[...])
    o = MOBA_WIDTH + MOBA_KV_WIDTH
    vm_ref[...] = z[:, o:o + MOBA_KV_WIDTH]
    o += MOBA_KV_WIDTH
    cqn = _rms(z[:, o:o + MLA_Q_LORA], gql_ref[...]).astype(BF16)
    o += MLA_Q_LORA
    ckv_ref[...] = _rms(z[:, o:o + MLA_KV_LORA], gkv_ref[...])
    kr_ref[...] = zk[:, :MLA_ROPE]

    qh = _dot(cqn, wuq_ref[...])
    rc, rs1, rs2 = rc_ref[...], rs1_ref[...], rs2_ref[...]
    gain = gmq_ref[...]
    kgn = kgn_ref[...]
    for h in range(MLA_HEADS):
        blk = qh[:, h * LANES:(h + 1) * LANES]
        ss = jnp.sum(blk * blk, axis=-1, keepdims=True)
        y = blk * lax.rsqrt(ss / MLA_QK + EPS) * gain
        y = y * rc + pltpu.roll(y, MLA_HALF, axis=1) * rs1 + pltpu.roll(y, LANES - MLA_HALF, axis=1) * rs2
        qmla_ref[:, h * LANES:(h + 1) * LANES] = y * kgn


def _inproj(x2d, tabs, n_tab, w):
    rows = x2d.shape[0]
    tm = min(ROW_TILE, rows)
    assert rows % tm == 0 and tm % SUBLANES == 0
    row_spec = lambda width: pl.BlockSpec((tm, width), lambda i: (i, 0))
    tab_spec = pl.BlockSpec((tm, LANES), lambda i: (i % n_tab, 0))
    consts = [w['ln1'], w['w_main'], w['w_kr'], w['gq'], w['gk'], w['gql'], w['w_uq'], w['gmq'], w['kgn'], w['gkv']]
    out_widths = (MOBA_WIDTH, MOBA_KV_WIDTH, MOBA_KV_WIDTH, MLA_HEADS * LANES, MLA_KV_LORA, MLA_ROPE)
    return pl.pallas_call(
        _inproj_kernel,
        grid=(rows // tm,),
        in_specs=[row_spec(D_MODEL)] + [_const_spec(c.shape) for c in consts] + [tab_spec] * 3,
        out_specs=[row_spec(wd) for wd in out_widths],
        out_shape=[jax.ShapeDtypeStruct((rows, wd), F32) for wd in out_widths],
        compiler_params=pltpu.CompilerParams(dimension_semantics=("parallel",), vmem_limit_bytes=VMEM_LIMIT_BYTES),
        name="inproj",
    )(x2d, *consts, *tabs)


def _moba_prompt_kernel(q_ref, k_ref, vt_ref, slope_ref, o_ref, kmean_ref, *, nb):
    i = pl.program_id(1)
    nbp = kmean_ref.shape[0]

    @pl.when(i == 0)
    def _():
        kmean_ref[...] = jnp.zeros_like(kmean_ref)
        for n in range(nb):
            kmean_ref[n:n + 1, :] = jnp.mean(k_ref[0, n * MOBA_BLOCK:(n + 1) * MOBA_BLOCK, :], axis=0, keepdims=True)

    scale = HEAD_DIM ** -0.5
    hpg = MOBA_HEADS // MOBA_KV_HEADS
    nqc = hpg * Q_TILE
    krow = lax.broadcasted_iota(jnp.int32, (KEY_TILE, nqc), 0)
    qcol = lax.broadcasted_iota(jnp.int32, (KEY_TILE, nqc), 1) % Q_TILE
    rel = (qcol - krow).astype(F32)
    blkrow = lax.broadcasted_iota(jnp.int32, (nbp, nqc), 0)
    past = blkrow < i
    second_head = lax.broadcasted_iota(jnp.int32, (1, nqc), 1) >= Q_TILE
    own_start = pl.multiple_of(i * MOBA_BLOCK, MOBA_BLOCK)

    for g in range(MOBA_KV_HEADS):
        kv_cols = slice(g * HEAD_DIM, (g + 1) * HEAD_DIM)
        q2 = jnp.concatenate([q_ref[0, :, (hpg * g + e) * HEAD_DIM:(hpg * g + e + 1) * HEAD_DIM] for e in range(hpg)],
                             axis=0)
        gate = _dot_nt_f32(kmean_ref[:, kv_cols], q2)
        sel = jnp.zeros((nbp, nqc), F32)
        for n in range(nb):
            gn = gate[n:n + 1, :]
            beats = ((gate > gn) | ((gate == gn) & (blkrow < n))) & past
            rank = jnp.sum(beats.astype(F32), axis=0, keepdims=True)
            sel = jnp.where((blkrow == n) & (rank < MOBA_TOPK) & past, 1.0, sel)
        slope_row = jnp.where(second_head, slope_ref[hpg * g + 1], slope_ref[hpg * g])
        bias = slope_row * rel
        qs = (q2 * scale).astype(BF16)

        def past_block(n, carry):
            m, l, acc = carry
            start = pl.multiple_of(n * MOBA_BLOCK, MOBA_BLOCK)
            kb = k_ref[0, pl.ds(start, MOBA_BLOCK), kv_cols].astype(BF16)
            vtb = vt_ref[0, n, kv_cols, :].astype(BF16)
            t_ = _dot_nt(kb, qs) - bias
            shift = slope_row * ((i - n) * MOBA_BLOCK).astype(F32)
            seln = jnp.sum(jnp.where(blkrow == n, sel, 0.0), axis=0, keepdims=True) > 0.5
            m_new = jnp.where(seln, jnp.maximum(m, jnp.max(t_, axis=0, keepdims=True) - shift), m)
            p = jnp.exp(jnp.minimum(t_ - (m_new + shift), 0.0))
            alpha = jnp.exp(m - m_new)
            l = alpha * l + jnp.where(seln, jnp.sum(p, axis=0, keepdims=True), 0.0)
            acc = alpha * acc + jnp.where(seln, _dot(vtb, p.astype(BF16)), 0.0)
            return m_new, l, acc

        init = (jnp.full((1, nqc), NEG, F32), jnp.zeros((1, nqc), F32), jnp.zeros((HEAD_DIM, nqc), F32))
        m, l, acc = lax.fori_loop(0, i, past_block, init)

        kb = k_ref[0, pl.ds(own_start, MOBA_BLOCK), kv_cols].astype(BF16)
        vtb = vt_ref[0, i, kv_cols, :].astype(BF16)
        t_ = jnp.where(rel >= 0.0, _dot_nt(kb, qs) - bias, NEG)
        m_new = jnp.maximum(m, jnp.max(t_, axis=0, keepdims=True))
        p = jnp.exp(t_ - m_new)
        alpha = jnp.exp(m - m_new)
        l = alpha * l + jnp.sum(p, axis=0, keepdims=True)
        out = (alpha * acc + _dot(vtb, p.astype(BF16))) / l
        for e in range(hpg):
            h = hpg * g + e
            o_ref[0, h * HEAD_DIM:(h + 1) * HEAD_DIM, :] = out[:, e * Q_TILE:(e + 1) * Q_TILE]


def _moba_prompt(qm, km, vm, slopes):
    b, t, _ = qm.shape
    assert t % MOBA_BLOCK == 0
    nb = t // MOBA_BLOCK
    assert nb >= MOBA_TOPK
    nbp = -(-nb // SUBLANES) * SUBLANES
    vt = jnp.transpose(vm.reshape(b, nb, MOBA_BLOCK, MOBA_KV_WIDTH), (0, 1, 3, 2))
    return pl.pallas_call(
        functools.partial(_moba_prompt_kernel, nb=nb),
        grid=(b, nb),
        in_specs=[pl.BlockSpec((1, Q_TILE, MOBA_WIDTH), lambda bi, i: (bi, i, 0)),
                  pl.BlockSpec((1, t, MOBA_KV_WIDTH), lambda bi, i: (bi, 0, 0)),
                  pl.BlockSpec((1, nb, MOBA_KV_WIDTH, MOBA_BLOCK), lambda bi, i: (bi, 0, 0, 0)),
                  pl.BlockSpec(memory_space=pltpu.SMEM)],
        out_specs=pl.BlockSpec((1, MOBA_WIDTH, Q_TILE), lambda bi, i: (bi, 0, i)),
        out_shape=jax.ShapeDtypeStruct((b, MOBA_WIDTH, t), F32),
        scratch_shapes=[pltpu.VMEM((nbp, MOBA_KV_WIDTH), F32)],
        compiler_params=pltpu.CompilerParams(dimension_semantics=("parallel", "arbitrary"),
                                             vmem_limit_bytes=VMEM_LIMIT_BYTES),
        name="moba_prompt",
    )(qm, km, vt, slopes)


def _mla_key_tile(c_bf, krT, wukT, kgr, cos, sin):
    kT = _dot_nt(wukT, c_bf)
    rot, inv = _mla_key_norm_rope(kT, krT, kgr, cos, sin)
    return kT, rot, inv


def _mla_key_norm_rope(kT, krT, kgr, cos, sin):
    n_keys = kT.shape[1]
    kn = kT[:MLA_HEADS * MLA_NOPE]
    ssq = jnp.sum((kn * kn).reshape(MLA_HEADS, MLA_NOPE, n_keys), axis=1)
    krsq = jnp.sum(krT * krT, axis=0, keepdims=True)
    inv = lax.rsqrt((ssq + krsq) / MLA_QK + EPS) * (MLA_QK ** -0.5)
    x = krT * kgr
    x1, x2 = x[:MLA_HALF], x[MLA_HALF:]
    rot = jnp.concatenate([x1 * cos - x2 * sin, x1 * sin + x2 * cos], axis=0)
    return rot, inv


def _mla_prompt_kernel(q_ref, c_ref, krT_ref, wkvT_ref, kgr_ref, cos_ref, sin_ref,
                       o_ref, khat_ref, vhat_ref, *, nb):
    i = pl.program_id(1)
    n_hd = MLA_HEADS * MLA_NOPE

    @pl.when(i == 0)
    def _():
        for n in range(nb):
            rows = slice(n * KEY_TILE, (n + 1) * KEY_TILE)
            cb = c_ref[0, rows, :].astype(BF16)
            kvT, rot, inv = _mla_key_tile(cb, krT_ref[0, :, rows], wkvT_ref[...], kgr_ref[...],
                                          cos_ref[:, rows], sin_ref[:, rows])
            pad = jnp.zeros((LANES - MLA_QK, KEY_TILE), F32)
            for h in range(MLA_HEADS):
                sc = inv[h:h + 1]
                kh = jnp.concatenate([kvT[h * MLA_NOPE:(h + 1) * MLA_NOPE] * sc, rot * sc, pad], axis=0)
                khat_ref[h, n] = kh.T.astype(BF16)
                vhat_ref[h, n] = kvT[n_hd + h * MLA_V:n_hd + (h + 1) * MLA_V].astype(BF16)

    krow = lax.broadcasted_iota(jnp.int32, (KEY_TILE, Q_TILE), 0)
    qcol = lax.broadcasted_iota(jnp.int32, (KEY_TILE, Q_TILE), 1)
    causal = krow <= qcol
    heads_per_iter = 2

    def step(h, n, qb, carry, mask):
        m, l, acc = carry
        s = _dot_nt(khat_ref[h, n], qb)
        if mask is not None:
            s = jnp.where(mask, s, NEG)
        m_new = jnp.maximum(m, jnp.max(s, axis=0, keepdims=True))
        alpha = jnp.exp(m - m_new)
        p = jnp.exp(s - m_new)
        l = alpha * l + jnp.sum(p, axis=0, keepdims=True)
        acc = alpha * acc + _dot(vhat_ref[h, n], p.astype(BF16))
        return m_new, l, acc

    for h0 in range(0, MLA_HEADS, heads_per_iter):
        heads = range(h0, h0 + heads_per_iter)
        qbs = [q_ref[0, :, h * LANES:(h + 1) * LANES].astype(BF16) for h in heads]

        def past_block(n, carries):
            return tuple(step(h, n, qb, c, None) for h, qb, c in zip(heads, qbs, carries))

        init = (jnp.full((1, Q_TILE), NEG, F32), jnp.zeros((1, Q_TILE), F32), jnp.zeros((MLA_V, Q_TILE), F32))
        carries = lax.fori_loop(0, i, past_block, (init,) * heads_per_iter)
        for h, qb, c in zip(heads, qbs, carries):
            _, l, acc = step(h, i, qb, c, causal)
            o_ref[0, h * MLA_V:(h + 1) * MLA_V, :] = acc / l


def _mla_prompt(qmla, ckv, krT, w, cos_t, sin_t):
    b, t, _ = qmla.shape
    assert t % KEY_TILE == 0
    nb = t // KEY_TILE
    consts = [w['w_kvT'], w['kgr'], cos_t, sin_t]
    return pl.pallas_call(
        functools.partial(_mla_prompt_kernel, nb=nb),
        grid=(b, nb),
        in_specs=[pl.BlockSpec((1, Q_TILE, MLA_HEADS * LANES), lambda bi, i: (bi, i, 0)),
                  pl.BlockSpec((1, t, MLA_KV_LORA), lambda bi, i: (bi, 0, 0)),
                  pl.BlockSpec((1, MLA_ROPE, t), lambda bi, i: (bi, 0, 0))]
                 + [_const_spec(c.shape) for c in consts],
        out_specs=pl.BlockSpec((1, MLA_WIDTH, Q_TILE), lambda bi, i: (bi, 0, i)),
        out_shape=jax.ShapeDtypeStruct((b, MLA_WIDTH, t), F32),
        scratch_shapes=[pltpu.VMEM((MLA_HEADS, nb, KEY_TILE, LANES), BF16),
                        pltpu.VMEM((MLA_HEADS, nb, MLA_V, KEY_TILE), BF16)],
        compiler_params=pltpu.CompilerParams(dimension_semantics=("parallel", "arbitrary"),
                                             vmem_limit_bytes=VMEM_LIMIT_BYTES),
        name="mla_prompt",
    )(qmla, ckv, krT, *consts)


class _PagedFetch:
    def __init__(self, pt_ref, pools, bufs, sem, pps):
        self.pt_ref, self.pools, self.bufs, self.sem, self.pps = pt_ref, pools, bufs, sem, pps
        self.t = pl.program_id(0) * pl.num_programs(1) + pl.program_id(1)
        self.last = pl.num_programs(0) * pl.num_programs(1) - 1
        self.slot = self.t % 2

    def _copies(self, step, slot, lookup):
        out = []
        for kk in range(self.pps):
            page = self.pt_ref[step * self.pps + kk] if lookup else 0
            for which, (pool, buf) in enumerate(zip(self.pools, self.bufs)):
                out.append(pltpu.make_async_copy(pool.at[page], buf.at[slot, kk], self.sem.at[slot, which]))
        return out

    def prime(self):
        @pl.when(self.t == 0)
        def _():
            for c in self._copies(0, 0, True):
                c.start()

    def prefetch_and_wait(self):
        for c in self._copies(jnp.minimum(self.t + 1, self.last), 1 - self.slot, True):
            c.start()
        for c in self._copies(self.t, self.slot, False):
            c.wait()

    def drain(self):
        @pl.when(self.t == self.last)
        def _():
            for c in self._copies(self.t, 1 - self.slot, False):
                c.wait()


def _moba_decode_kernel(pt_ref, qbd_ref, knew_ref, vnew_ref, slope_ref, tq_ref, kt_hbm, vt_hbm, o_ref,
                        gate_s, m_s, l_s, o_s, kbuf, vbuf, sem, *, pps, past_len):
    j = pl.program_id(1)
    nblk = gate_s.shape[0]
    nq = qbd_ref.shape[1]
    scale = HEAD_DIM ** -0.5
    bps = pps // 2
    fetch = _PagedFetch(pt_ref, (kt_hbm, vt_hbm), (kbuf, vbuf), sem, pps)
    fetch.prime()
    fetch.prefetch_and_wait()
    slot = fetch.slot

    qbd = qbd_ref[0]
    q_hi, q_lo = _split_bf16(qbd)
    qb2 = jnp.concatenate([q_hi, q_lo], axis=0)
    qb = q_hi
    slope = slope_ref[...]
    tq = tq_ref[...]
    slope2 = jnp.concatenate([slope, slope], axis=1)
    tq2 = jnp.concatenate([tq, tq], axis=1)
    lane2 = lax.broadcasted_iota(jnp.int32, (nq, MOBA_BLOCK), 1).astype(F32)

    def page_t(buf, kk):
        return buf[slot, kk].reshape(MOBA_KV_WIDTH, buf.shape[-1]).astype(BF16)

    for bb in range(bps):
        ktb = jnp.concatenate([page_t(kbuf, 2 * bb), page_t(kbuf, 2 * bb + 1)], axis=1)
        vtb = jnp.concatenate([page_t(vbuf, 2 * bb), page_t(vbuf, 2 * bb + 1)], axis=1)
        blk = j * bps + bb
        dist = (past_len + tq2) - ((blk * MOBA_BLOCK).astype(F32) + lane2)
        qk2 = _dot(qb2, ktb)
        qk = qk2[:nq]
        gate = jnp.sum(qk + qk2[nq:], axis=-1, keepdims=True) / MOBA_BLOCK
        s = qk * scale - slope2 * dist
        m = jnp.max(s, axis=-1, keepdims=True)
        p = jnp.exp(s - m)
        l = jnp.sum(p, axis=-1, keepdims=True)
        gate_s[blk] = jnp.broadcast_to(gate, (nq, LANES))
        m_s[blk] = jnp.broadcast_to(m, (nq, LANES))
        l_s[blk] = jnp.broadcast_to(l, (nq, LANES))
        o_s[blk] = _dot_nt(p.astype(BF16), vtb)

    @pl.when(j == pl.num_programs(1) - 1)
    def _():
        gates = gate_s[...]
        bidx = lax.broadcasted_iota(jnp.int32, gates.shape, 0)
        picked = jnp.zeros(gates.shape, F32)
        for _ in range(min(MOBA_TOPK, nblk)):
            mx = jnp.max(gates, axis=0, keepdims=True)
            first = jnp.min(jnp.where(gates == mx, bidx, nblk), axis=0, keepdims=True)
            hit = bidx == first
            picked = jnp.where(hit, 1.0, picked)
            gates = jnp.where(hit, -jnp.inf, gates)
        sel = picked > 0.5

        n_new = knew_ref.shape[1]
        tk = lax.broadcasted_iota(jnp.int32, (nq, n_new), 1).astype(F32)
        tq_n = tq[:, :n_new]
        dist_o = tq_n - tk
        s_o = _dot_nt(qb, knew_ref[0].astype(BF16)) * scale - slope[:, :n_new] * dist_o
        mask_o = dist_o >= 0.0
        s_o = jnp.where(mask_o, s_o, NEG)
        m_o = jnp.broadcast_to(jnp.max(s_o, axis=-1, keepdims=True), (nq, LANES))

        m_all = m_s[...]
        m_fin = jnp.maximum(jnp.max(jnp.where(sel, m_all, NEG), axis=0), m_o)
        wgt = jnp.exp(jnp.where(sel, m_all - m_fin[None], -jnp.inf))
        p_o = jnp.where(mask_o, jnp.exp(s_o - m_fin[:, :n_new]), 0.0)
        l_fin = jnp.sum(wgt * l_s[...], axis=0) + jnp.sum(p_o, axis=-1, keepdims=True)
        wgt2 = jnp.concatenate([wgt, wgt], axis=2)
        o_fin = jnp.sum(wgt2 * o_s[...], axis=0) + _dot(p_o.astype(BF16), vnew_ref[0].astype(BF16))
        o_fin = o_fin / jnp.concatenate([l_fin, l_fin], axis=1)

        row = lax.broadcasted_iota(jnp.int32, (nq, HEAD_DIM), 0)
        grp = (row % MOBA_HEADS) // (MOBA_HEADS // MOBA_KV_HEADS)
        out = jnp.zeros((nq, HEAD_DIM), F32)
        for g in range(MOBA_KV_HEADS):
            out = jnp.where(grp == g, o_fin[:, g * HEAD_DIM:(g + 1) * HEAD_DIM], out)
        o_ref[0] = out

    fetch.drain()


def _pages_per_step(n_pages, target):
    pps = min(target, n_pages)
    assert n_pages % pps == 0 and pps % 2 == 0
    return pps


def _moba_decode(page_flat, qbd, knew, vnew, slope_rows, tq_rows, pool_kt, pool_vt, n_pages, page):
    s, nq, _ = qbd.shape
    assert 2 * page == MOBA_BLOCK and pool_kt.shape[1:] == (MOBA_KV_HEADS, HEAD_DIM, page)
    pps = _pages_per_step(n_pages, MOBA_DECODE_PAGES)
    nblk = n_pages // 2
    n_new = knew.shape[1]

    per_seq = lambda shape: pl.BlockSpec((1,) + shape, lambda si, j, pt: (si, 0, 0))
    const2 = pl.BlockSpec((nq, LANES), lambda si, j, pt: (0, 0))
    hbm = pl.BlockSpec(memory_space=pl.ANY)
    page_buf = pltpu.VMEM((2, pps, MOBA_KV_HEADS, HEAD_DIM, page), F32)
    grid_spec = pltpu.PrefetchScalarGridSpec(
        num_scalar_prefetch=1,
        grid=(s, n_pages // pps),
        in_specs=[per_seq((nq, MOBA_KV_WIDTH)), per_seq((n_new, MOBA_KV_WIDTH)), per_seq((n_new, MOBA_KV_WIDTH)),
                  const2, const2, hbm, hbm],
        out_specs=per_seq((nq, HEAD_DIM)),
        scratch_shapes=[pltpu.VMEM((nblk, nq, LANES), F32)] * 3 + [pltpu.VMEM((nblk, nq, MOBA_KV_WIDTH), F32)]
                       + [page_buf, page_buf, pltpu.SemaphoreType.DMA((2, 2))],
    )
    return pl.pallas_call(
        functools.partial(_moba_decode_kernel, pps=pps, past_len=float(n_pages * page)),
        grid_spec=grid_spec,
        out_shape=jax.ShapeDtypeStruct((s, nq, HEAD_DIM), F32),
        compiler_params=pltpu.CompilerParams(dimension_semantics=("arbitrary", "arbitrary"),
                                             vmem_limit_bytes=VMEM_LIMIT_BYTES),
        name="moba_decode",
    )(page_flat, qbd, knew, vnew, slope_rows, tq_rows, pool_kt, pool_vt)


def _mla_decode_kernel(pt_ref, qbdn_ref, qr_ref, cnew_ref, krnew_ref, tq_ref, wukT_ref, wuv_ref, kgr_ref,
                       cos_ref, sin_ref, cosn_ref, sinn_ref, c_hbm, rt_hbm, o_ref,
                       lhs_s, m_s, l_s, acc_s, cbuf, rbuf, sem, *, pps, sub, n_tok):
    j = pl.program_id(1)
    nq = qr_ref.shape[1]
    page = cbuf.shape[2]
    n_hd = MLA_HEADS * MLA_NOPE
    fetch = _PagedFetch(pt_ref, (c_hbm, rt_hbm), (cbuf, rbuf), sem, pps)
    fetch.prime()

    @pl.when(j == 0)
    def _():
        lhs_s[:n_hd, :] = wukT_ref[...]
        lhs_s[n_hd:, :] = _dot(qbdn_ref[0].astype(BF16), wukT_ref[...]).astype(BF16)

    fetch.prefetch_and_wait()
    buf_slot = fetch.slot
    qr = qr_ref[0].astype(BF16)

    def partial_softmax(cb, krT, cos, sin, mask, kT=None):
        if kT is None:
            kT = _dot_nt(lhs_s[...], cb)
        rot, inv = _mla_key_norm_rope(kT, krT, kgr_ref[...], cos, sin)
        inv_rows = jnp.concatenate([inv] * (nq // MLA_HEADS), axis=0)
        s = (kT[n_hd:] + _dot(qr, rot.astype(BF16))) * inv_rows
        if mask is not None:
            s = jnp.where(mask, s, NEG)
        m = jnp.max(s, axis=-1, keepdims=True)
        p = jnp.exp(s - m)
        if mask is not None:
            p = jnp.where(mask, p, 0.0)
        return m, jnp.sum(p, axis=-1, keepdims=True), _dot(p.astype(BF16), cb)

    n_sub = pps // sub
    latent = lambda sc: jnp.concatenate([cbuf[buf_slot, k] for k in range(sc * sub, (sc + 1) * sub)],
                                        axis=0).astype(BF16)
    cb_next = latent(0)
    kT_next = _dot_nt(lhs_s[...], cb_next)
    for sc in range(n_sub):
        cb, kT = cb_next, kT_next
        if sc + 1 < n_sub:
            cb_next = latent(sc + 1)
            kT_next = _dot_nt(lhs_s[...], cb_next)
        krT = jnp.concatenate([rbuf[buf_slot, k] for k in range(sc * sub, (sc + 1) * sub)], axis=1)
        cols = slice(sc * sub * page, (sc + 1) * sub * page)
        m, l, acc = partial_softmax(cb, krT, cos_ref[:, cols], sin_ref[:, cols], None, kT)
        part = j * (pps // sub) + sc
        m_s[part] = jnp.broadcast_to(m, (nq, LANES))
        l_s[part] = jnp.broadcast_to(l, (nq, LANES))
        acc_s[part] = acc

    @pl.when(j == pl.num_programs(1) - 1)
    def _():
        tk = lax.broadcasted_iota(jnp.int32, (nq, LANES), 1).astype(F32)
        mask = (tk <= tq_ref[...]) & (tk < n_tok)
        m_n, l_n, acc_n = partial_softmax(cnew_ref[0].astype(BF16), krnew_ref[0], cosn_ref[...], sinn_ref[...], mask)
        m_all = m_s[...]
        m_fin = jnp.maximum(jnp.max(m_all, axis=0), m_n)
        wgt = jnp.exp(m_all - m_fin[None])
        w_n = jnp.exp(m_n - m_fin)
        l_fin = jnp.sum(wgt * l_s[...], axis=0) + w_n * l_n
        a_fin = jnp.sum(wgt * acc_s[...], axis=0) + w_n * acc_n
        o_lat = (a_fin / l_fin).astype(BF16)
        full = _dot(o_lat, wuv_ref[...])
        row = lax.broadcasted_iota(jnp.int32, (nq, MLA_V), 0)
        head = row % MLA_HEADS
        out = jnp.zeros((nq, MLA_V), F32)
        for h in range(MLA_HEADS):
            out = jnp.where(head == h, full[:, h * MLA_V:(h + 1) * MLA_V], out)
        o_ref[0] = out

    fetch.drain()


def _mla_decode(page_flat, qbdn, qr, cnew, krnew_t, tq_rows, w, cos_p, sin_p, cos_n, sin_n,
                pool_c, pool_rt, n_pages, page, n_tok):
    s, nq, _ = qr.shape
    pps = _pages_per_step(n_pages, MLA_DECODE_PAGES)
    sub = min(MLA_DECODE_SUB_PAGES, pps)
    assert page == LANES and pps % sub == 0
    n_slots = n_pages // sub

    per_seq = lambda shape: pl.BlockSpec((1,) + shape, lambda si, j, pt: (si, 0, 0))
    cst = lambda a: pl.BlockSpec(a.shape, lambda si, j, pt: (0,) * a.ndim)
    tab = pl.BlockSpec((MLA_HALF, pps * page), lambda si, j, pt: (0, j))
    hbm = pl.BlockSpec(memory_space=pl.ANY)
    n_lhs = MLA_HEADS * MLA_NOPE + nq
    grid_spec = pltpu.PrefetchScalarGridSpec(
        num_scalar_prefetch=1,
        grid=(s, n_pages // pps),
        in_specs=[per_seq((nq, MLA_HEADS * MLA_NOPE)), per_seq((nq, MLA_ROPE)),
                  per_seq((LANES, MLA_KV_LORA)), per_seq((MLA_ROPE, LANES)), cst(tq_rows),
                  cst(w['w_ukT']), cst(w['w_uv']), cst(w['kgr']), tab, tab, cst(cos_n), cst(sin_n), hbm, hbm],
        out_specs=per_seq((nq, MLA_V)),
        scratch_shapes=[pltpu.VMEM((n_lhs, MLA_KV_LORA), BF16), pltpu.VMEM((n_slots, nq, LANES), F32),
                        pltpu.VMEM((n_slots, nq, LANES), F32), pltpu.VMEM((n_slots, nq, MLA_KV_LORA), F32),
                        pltpu.VMEM((2, pps, page, MLA_KV_LORA), F32), pltpu.VMEM((2, pps, MLA_ROPE, page), F32),
                        pltpu.SemaphoreType.DMA((2, 2))],
    )
    return pl.pallas_call(
        functools.partial(_mla_decode_kernel, pps=pps, sub=sub, n_tok=float(n_tok)),
        grid_spec=grid_spec,
        out_shape=jax.ShapeDtypeStruct((s, nq, MLA_V), F32),
        compiler_params=pltpu.CompilerParams(dimension_semantics=("arbitrary", "arbitrary"),
                                             vmem_limit_bytes=VMEM_LIMIT_BYTES),
        name="mla_decode",
    )(page_flat, qbdn, qr, cnew, krnew_t, tq_rows, w['w_ukT'], w['w_uv'], w['kgr'], cos_p, sin_p, cos_n, sin_n,
      pool_c, pool_rt)


def _out_mlp_kernel(x_ref, om_ref, ol_ref, gom_ref, gol_ref, wo_ref, ln2_ref, wup_ref, wdn_ref, y_ref):
    mixed = jnp.concatenate([_rms(om_ref[...], gom_ref[...]), _rms(ol_ref[...], gol_ref[...])], axis=-1)
    x2 = x_ref[...] + _dot(mixed.astype(BF16), wo_ref[...])
    hb = _rms(x2, ln2_ref[...]).astype(BF16)
    acc = x2
    for c in range(D_FF // D_MODEL):
        cols = slice(c * D_MODEL, (c + 1) * D_MODEL)
        u = jnp.maximum(_dot(hb, wup_ref[:, cols]), 0.0)
        acc = acc + _dot((u * u).astype(BF16), wdn_ref[cols, :])
    y_ref[...] = acc


def _out_mlp(x2d, om, ol, w):
    rows = x2d.shape[0]
    tm = min(ROW_TILE, rows)
    assert rows % tm == 0
    row_spec = lambda width: pl.BlockSpec((tm, width), lambda i: (i, 0))
    single = lambda a: pl.BlockSpec(a.shape, lambda i: (0,) * a.ndim, pipeline_mode=pl.Buffered(1))
    consts = [w['gom'], w['gol'], w['w_o'], w['ln2'], w['w_up'], w['w_down']]
    return pl.pallas_call(
        _out_mlp_kernel,
        grid=(rows // tm,),
        in_specs=[row_spec(D_MODEL), row_spec(MOBA_WIDTH), row_spec(MLA_WIDTH)] + [single(c) for c in consts],
        out_specs=row_spec(D_MODEL),
        out_shape=jax.ShapeDtypeStruct((rows, D_MODEL), F32),
        compiler_params=pltpu.CompilerParams(dimension_semantics=("parallel",), vmem_limit_bytes=VMEM_LIMIT_BYTES),
        name="out_mlp",
    )(x2d, om, ol, *consts)


def _rope_angles(pos):
    inv_freq = ROPE_BASE ** (-jnp.arange(MLA_HALF, dtype=F32) / MLA_HALF)
    ang = pos.astype(F32)[:, None] * inv_freq[None, :]
    return jnp.cos(ang), jnp.sin(ang)


def _query_rope_tables(pos):
    cos, sin = _rope_angles(pos)
    n = pos.shape[0]
    rc = jnp.concatenate([jnp.ones((n, MLA_NOPE), F32), cos, cos, jnp.zeros((n, LANES - MLA_QK), F32)], axis=1)
    zeros = lambda k: jnp.zeros((n, k), F32)
    rs1 = jnp.concatenate([zeros(MLA_NOPE + MLA_HALF), sin, zeros(LANES - MLA_QK)], axis=1)
    rs2 = jnp.concatenate([zeros(MLA_NOPE), -sin, zeros(LANES - MLA_NOPE - MLA_HALF)], axis=1)
    return rc, rs1, rs2


def _head_block(v):
    return jnp.concatenate([v, jnp.zeros((LANES - MLA_QK,), F32)])[None, :]


def _prep_weights(ln1_g, w_in, moba_q_g, moba_k_g, mla_q_lora_g, w_uq, mla_q_g, mla_kv_lora_g, w_uk, w_uv,
                  mla_k_g, out_g_moba, out_g_mla, w_o, ln2_g, w_up, w_down):
    w_uq_h = w_uq.reshape(MLA_Q_LORA, MLA_HEADS, MLA_QK)
    w_uq_h = jnp.pad(w_uq_h, ((0, 0), (0, 0), (0, LANES - MLA_QK))).reshape(MLA_Q_LORA, MLA_HEADS * LANES)
    kgn = jnp.concatenate([mla_k_g[:MLA_NOPE], jnp.ones((LANES - MLA_NOPE,), F32)])[None, :]
    return {
        'ln1': ln1_g[None, :],
        'w_main': w_in[:, :MAIN_WIDTH].astype(BF16),
        'w_kr': jnp.pad(w_in[:, MAIN_WIDTH:], ((0, 0), (0, LANES - MLA_ROPE))).astype(BF16),
        'gq': jnp.tile(moba_q_g, 2)[None, :],
        'gk': jnp.tile(moba_k_g, 2)[None, :],
        'gql': mla_q_lora_g[None, :],
        'w_uq': w_uq_h.astype(BF16),
        'gmq': _head_block(mla_q_g),
        'kgn': kgn,
        'gkv': mla_kv_lora_g[None, :],
        'w_ukT': w_uk.reshape(MLA_KV_LORA, MLA_HEADS * MLA_NOPE).T.astype(BF16),
        'w_uv': w_uv.reshape(MLA_KV_LORA, MLA_HEADS * MLA_V).astype(BF16),
        'w_kvT': jnp.concatenate([w_uk.reshape(MLA_KV_LORA, MLA_HEADS * MLA_NOPE).T,
                                  w_uv.reshape(MLA_KV_LORA, MLA_HEADS * MLA_V).T]).astype(BF16),
        'kgr': mla_k_g[MLA_NOPE:][:, None],
        'gom': out_g_moba[None, :],
        'gol': out_g_mla[None, :],
        'w_o': w_o.astype(BF16),
        'ln2': ln2_g[None, :],
        'w_up': w_up.astype(BF16),
        'w_down': w_down.astype(BF16),
    }


def _alibi_slopes():
    return jnp.exp2(-8.0 * jnp.arange(1, MOBA_HEADS + 1, dtype=F32) / MOBA_HEADS)


def _prompt_layer(x, w):
    b, t, _ = x.shape
    pos = jnp.arange(t, dtype=jnp.int32)
    x2d = x.reshape(b * t, D_MODEL)
    tm = min(ROW_TILE, b * t)
    assert t % tm == 0
    qm, km, vm, qmla, ckv, kr = _inproj(x2d, _query_rope_tables(pos), t // tm, w)
    o_moba_t = _moba_prompt(qm.reshape(b, t, -1), km.reshape(b, t, -1), vm.reshape(b, t, -1), _alibi_slopes())
    o_moba = jnp.transpose(o_moba_t, (0, 2, 1))
    cos, sin = _rope_angles(pos)
    krT = jnp.transpose(kr.reshape(b, t, -1), (0, 2, 1))
    o_mla_t = _mla_prompt(qmla.reshape(b, t, -1), ckv.reshape(b, t, -1), krT, w, cos.T, sin.T)
    o_mla = jnp.transpose(o_mla_t, (0, 2, 1))
    y = _out_mlp(x2d, o_moba.reshape(b * t, -1), o_mla.reshape(b * t, -1), w)
    return (y.reshape(b, t, D_MODEL), km.reshape(b, t, MOBA_KV_HEADS, HEAD_DIM),
            vm.reshape(b, t, MOBA_KV_HEADS, HEAD_DIM), ckv.reshape(b, t, MLA_KV_LORA), kr.reshape(b, t, MLA_ROPE))


def _pad_rows(a, n):
    return jnp.pad(a, ((0, 0), (0, n - a.shape[1]), (0, 0)))


def _sample_layer(x, w, pool_k, pool_v, pool_c, pool_r, page_table):
    s, t, _ = x.shape
    n_pages = page_table.shape[1]
    page = pool_k.shape[1]
    past_len = n_pages * page
    nq = t * MOBA_HEADS
    x2d = x.reshape(s * t, D_MODEL)
    tm = min(ROW_TILE, s * t)
    assert tm % t == 0
    pos_new = past_len + jnp.arange(t, dtype=jnp.int32)
    qm, km, vm, qmla, ckv, kr = _inproj(x2d, _query_rope_tables(jnp.tile(pos_new, tm // t)), 1, w)

    page_flat = page_table.reshape(-1)
    row = jnp.arange(nq)
    tq_rows = jnp.broadcast_to((row // MOBA_HEADS).astype(F32)[:, None], (nq, LANES))
    slope_rows = jnp.broadcast_to(_alibi_slopes()[row % MOBA_HEADS][:, None], (nq, LANES))

    kv_onehot = jax.nn.one_hot(jnp.arange(MOBA_HEADS) // (MOBA_HEADS // MOBA_KV_HEADS), MOBA_KV_HEADS, dtype=F32)
    qbd = (qm.reshape(s, t, MOBA_HEADS, 1, HEAD_DIM) * kv_onehot[None, None, :, :, None]).reshape(s, nq, MOBA_KV_WIDTH)
    n_new = -(-t // (2 * SUBLANES)) * (2 * SUBLANES)
    page_t = lambda pool: jnp.transpose(pool, (0, 2, 3, 1))
    o_moba = _moba_decode(page_flat, qbd, _pad_rows(km.reshape(s, t, -1), n_new), _pad_rows(vm.reshape(s, t, -1), n_new),
                          slope_rows, tq_rows, page_t(pool_k), page_t(pool_v), n_pages, page)

    qh = qmla.reshape(s, t, MLA_HEADS, LANES)
    head_eye = jnp.eye(MLA_HEADS, dtype=F32)
    qbdn = (qh[..., None, :MLA_NOPE] * head_eye[None, None, :, :, None]).reshape(s, nq, MLA_HEADS * MLA_NOPE)
    qr = qh[..., MLA_NOPE:MLA_QK].reshape(s, nq, MLA_ROPE)
    cos_p, sin_p = _rope_angles(jnp.arange(past_len, dtype=jnp.int32))
    cos_n, sin_n = _rope_angles(past_len + jnp.arange(LANES, dtype=jnp.int32))
    krnew_t = jnp.transpose(_pad_rows(kr.reshape(s, t, -1), LANES), (0, 2, 1))
    o_mla = _mla_decode(page_flat, qbdn, qr, _pad_rows(ckv.reshape(s, t, -1), LANES), krnew_t,
                        tq_rows, w, cos_p.T, sin_p.T, cos_n.T, sin_n.T, pool_c, jnp.transpose(pool_r, (0, 2, 1)),
                        n_pages, page, t)

    y = _out_mlp(x2d, o_moba.reshape(s * t, MOBA_WIDTH), o_mla.reshape(s * t, MLA_WIDTH), w)
    return (y.reshape(s, t, D_MODEL), km.reshape(s, t, MOBA_KV_HEADS, HEAD_DIM),
            vm.reshape(s, t, MOBA_KV_HEADS, HEAD_DIM), ckv.reshape(s, t, MLA_KV_LORA), kr.reshape(s, t, MLA_ROPE))


def kernel(x_prompt, x_sample, cache_moba_k, cache_moba_v, cache_mla_ckv, cache_mla_krope, page_table, ln1_g, w_in,
           moba_q_g, moba_k_g, mla_q_lora_g, w_uq, mla_q_g, mla_kv_lora_g, w_uk, w_uv, mla_k_g, out_g_moba,
           out_g_mla, w_o, ln2_g, w_up, w_down):
    depth = w_in.shape[0]
    layer_weights = (ln1_g, w_in, moba_q_g, moba_k_g, mla_q_lora_g, w_uq, mla_q_g, mla_kv_lora_g, w_uk, w_uv,
                     mla_k_g, out_g_moba, out_g_mla, w_o, ln2_g, w_up, w_down)
    y_p, y_s = x_prompt, x_sample
    outs_p, outs_s = [], []
    for layer in range(depth):
        w = _prep_weights(*(a[layer] for a in layer_weights))
        y_p, *new_p = _prompt_layer(y_p, w)
        y_s, *new_s = _sample_layer(y_s, w, cache_moba_k[layer], cache_moba_v[layer], cache_mla_ckv[layer],
                                    cache_mla_krope[layer], page_table)
        outs_p.append(new_p)
        outs_s.append(new_s)
    stack = lambda outs, k: jnp.stack([o[k] for o in outs])
    return (y_p, y_s, stack(outs_p, 0), stack(outs_p, 1), stack(outs_p, 2), stack(outs_p, 3),
            stack(outs_s, 0), stack(outs_s, 1), stack(outs_s, 2), stack(outs_s, 3))
```

```python
import functools

import jax
import jax.numpy as jnp
from jax import lax
from jax.experimental import pallas as pl
from jax.experimental.pallas import tpu as pltpu

F32 = jnp.float32
BF16 = jnp.bfloat16

D_MODEL = 1024
HEAD_DIM = 64
MOBA_HEADS = 8
MOBA_KV_HEADS = 4
MOBA_WIDTH = MOBA_HEADS * HEAD_DIM
MOBA_KV_WIDTH = MOBA_KV_HEADS * HEAD_DIM
MOBA_BLOCK = 256
MOBA_TOPK = 3
MLA_HEADS = 8
MLA_V = 64
MLA_WIDTH = MLA_HEADS * MLA_V
MLA_NOPE = 64
MLA_ROPE = 32
MLA_HALF = MLA_ROPE // 2
MLA_QK = MLA_NOPE + MLA_ROPE
MLA_Q_LORA = 384
MLA_KV_LORA = 128
ROPE_BASE = 10000.0
D_FF = 4 * D_MODEL
EPS = 1e-6
MAIN_WIDTH = MOBA_WIDTH + 2 * MOBA_KV_WIDTH + MLA_Q_LORA + MLA_KV_LORA

LANES = 128
SUBLANES = 8
VMEM_LIMIT_BYTES = 56 * 1024 * 1024

ROW_TILE = 512
Q_TILE = MOBA_BLOCK
KEY_TILE = 256
DECODE_PAGES = 32
MLA_DECODE_SUB_PAGES = 32
NEG = -1e30

_NT = (((1,), (1,)), ((), ()))


def _dot(a, b):
    return jnp.dot(a, b, preferred_element_type=F32)


def _dot_nt(a, b):
    return lax.dot_general(a, b, _NT, preferred_element_type=F32)


def _split_bf16(a):
    hi = a.astype(BF16)
    lo = (a - hi.astype(F32)).astype(BF16)
    return hi, lo


def _dot_nt_f32(a, b):
    ah, al = _split_bf16(a)
    bh, bl = _split_bf16(b)
    return _dot_nt(ah, bh) + _dot_nt(ah, bl) + _dot_nt(al, bh)


def _rms(x, g):
    return x * lax.rsqrt(jnp.mean(x * x, axis=-1, keepdims=True) + EPS) * g


def _const_spec(shape):
    nd = len(shape)
    return pl.BlockSpec(shape, lambda *_: (0,) * nd)


def _inproj_kernel(x_ref, ln1_ref, wmain_ref, wkr_ref, gq_ref, gk_ref, gql_ref, wuq_ref, gmq_ref,
                   kgn_ref, gkv_ref, rc_ref, rs1_ref, rs2_ref,
                   qm_ref, km_ref, vm_ref, qmla_ref, ckv_ref, kr_ref):
    x = x_ref[...]
    hb = _rms(x, ln1_ref[...]).astype(BF16)
    z = _dot(hb, wmain_ref[...])
    zk = _dot(hb, wkr_ref[...])
    tm = x.shape[0]
    lane = lax.broadcasted_iota(jnp.int32, (tm, LANES), 1)
    low = lane < HEAD_DIM

    def pair_norm(blk, g):
        sq = blk * blk
        lo = jnp.sum(jnp.where(low, sq, 0.0), axis=-1, keepdims=True)
        hi = jnp.sum(jnp.where(low, 0.0, sq), axis=-1, keepdims=True)
        inv = jnp.where(low, lax.rsqrt(lo / HEAD_DIM + EPS), lax.rsqrt(hi / HEAD_DIM + EPS))
        return blk * inv * g

    for j in range(MOBA_WIDTH // LANES):
        qm_ref[:, j * LANES:(j + 1) * LANES] = pair_norm(z[:, j * LANES:(j + 1) * LANES], gq_ref[...])
    for j in range(MOBA_KV_WIDTH // LANES):
        o = MOBA_WIDTH + j * LANES
        km_ref[:, j * LANES:(j + 1) * LANES] = pair_norm(z[:, o:o + LANES], gk_ref[...])
    o = MOBA_WIDTH + MOBA_KV_WIDTH
    vm_ref[...] = z[:, o:o + MOBA_KV_WIDTH]
    o += MOBA_KV_WIDTH
    cqn = _rms(z[:, o:o + MLA_Q_LORA], gql_ref[...]).astype(BF16)
    o += MLA_Q_LORA
    ckv_ref[...] = _rms(z[:, o:o + MLA_KV_LORA], gkv_ref[...])
    kr_ref[...] = zk[:, :MLA_ROPE]

    qh = _dot(cqn, wuq_ref[...])
    rc, rs1, rs2 = rc_ref[...], rs1_ref[...], rs2_ref[...]
    gain = gmq_ref[...]
    kgn = kgn_ref[...]
    for h in range(MLA_HEADS):
        blk = qh[:, h * LANES:(h + 1) * LANES]
        ss = jnp.sum(blk * blk, axis=-1, keepdims=True)
        y = blk * lax.rsqrt(ss / MLA_QK + EPS) * gain
        y = y * rc + pltpu.roll(y, MLA_HALF, axis=1) * rs1 + pltpu.roll(y, LANES - MLA_HALF, axis=1) * rs2
        qmla_ref[:, h * LANES:(h + 1) * LANES] = y * kgn


def _inproj(x2d, tabs, n_tab, w):
    rows = x2d.shape[0]
    tm = min(ROW_TILE, rows)
    assert rows % tm == 0 and tm % SUBLANES == 0
    row_spec = lambda width: pl.BlockSpec((tm, width), lambda i: (i, 0))
    tab_spec = pl.BlockSpec((tm, LANES), lambda i: (i % n_tab, 0))
    consts = [w['ln1'], w['w_main'], w['w_kr'], w['gq'], w['gk'], w['gql'], w['w_uq'], w['gmq'], w['kgn'], w['gkv']]
    out_widths = (MOBA_WIDTH, MOBA_KV_WIDTH, MOBA_KV_WIDTH, MLA_HEADS * LANES, MLA_KV_LORA, MLA_ROPE)
    return pl.pallas_call(
        _inproj_kernel,
        grid=(rows // tm,),
        in_specs=[row_spec(D_MODEL)] + [_const_spec(c.shape) for c in consts] + [tab_spec] * 3,
        out_specs=[row_spec(wd) for wd in out_widths],
        out_shape=[jax.ShapeDtypeStruct((rows, wd), F32) for wd in out_widths],
        compiler_params=pltpu.CompilerParams(dimension_semantics=("parallel",), vmem_limit_bytes=VMEM_LIMIT_BYTES),
        name="inproj",
    )(x2d, *consts, *tabs)


def _moba_prompt_kernel(q_ref, k_ref, vt_ref, slope_ref, o_ref, kmean_ref, *, nb):
    i = pl.program_id(1)
    nbp = kmean_ref.shape[0]

    @pl.when(i == 0)
    def _():
        kmean_ref[...] = jnp.zeros_like(kmean_ref)
        for n in range(nb):
            kmean_ref[n:n + 1, :] = jnp.mean(k_ref[0, n * MOBA_BLOCK:(n + 1) * MOBA_BLOCK, :], axis=0, keepdims=True)

    scale = HEAD_DIM ** -0.5
    hpg = MOBA_HEADS // MOBA_KV_HEADS
    nqc = hpg * Q_TILE
    krow = lax.broadcasted_iota(jnp.int32, (KEY_TILE, nqc), 0)
    qcol = lax.broadcasted_iota(jnp.int32, (KEY_TILE, nqc), 1) % Q_TILE
    rel = (qcol - krow).astype(F32)
    blkrow = lax.broadcasted_iota(jnp.int32, (nbp, nqc), 0)
    past = blkrow < i
    second_head = lax.broadcasted_iota(jnp.int32, (1, nqc), 1) >= Q_TILE
    own_start = pl.multiple_of(i * MOBA_BLOCK, MOBA_BLOCK)

    for g in range(MOBA_KV_HEADS):
        kv_cols = slice(g * HEAD_DIM, (g + 1) * HEAD_DIM)
        q2 = jnp.concatenate([q_ref[0, :, (hpg * g + e) * HEAD_DIM:(hpg * g + e + 1) * HEAD_DIM] for e in range(hpg)],
                             axis=0)
        gate = _dot_nt_f32(kmean_ref[:, kv_cols], q2)
        sel = jnp.zeros((nbp, nqc), F32)
        for n in range(nb):
            gn = gate[n:n + 1, :]
            beats = ((gate > gn) | ((gate == gn) & (blkrow < n))) & past
            rank = jnp.sum(beats.astype(F32), axis=0, keepdims=True)
            sel = jnp.where((blkrow == n) & (rank < MOBA_TOPK) & past, 1.0, sel)
        slope_row = jnp.where(second_head, slope_ref[hpg * g + 1], slope_ref[hpg * g])
        bias = slope_row * rel
        qs = (q2 * scale).astype(BF16)

        def past_block(n, carry):
            m, l, acc = carry
            start = pl.multiple_of(n * MOBA_BLOCK, MOBA_BLOCK)
            kb = k_ref[0, pl.ds(start, MOBA_BLOCK), kv_cols].astype(BF16)
            vtb = vt_ref[0, n, kv_cols, :].astype(BF16)
            t_ = _dot_nt(kb, qs) - bias
            shift = slope_row * ((i - n) * MOBA_BLOCK).astype(F32)
            seln = jnp.sum(jnp.where(blkrow == n, sel, 0.0), axis=0, keepdims=True) > 0.5
            m_new = jnp.where(seln, jnp.maximum(m, jnp.max(t_, axis=0, keepdims=True) - shift), m)
            p = jnp.exp(jnp.minimum(t_ - (m_new + shift), 0.0))
            alpha = jnp.exp(m - m_new)
            l = alpha * l + jnp.where(seln, jnp.sum(p, axis=0, keepdims=True), 0.0)
            acc = alpha * acc + jnp.where(seln, _dot(vtb, p.astype(BF16)), 0.0)
            return m_new, l, acc

        init = (jnp.full((1, nqc), NEG, F32), jnp.zeros((1, nqc), F32), jnp.zeros((HEAD_DIM, nqc), F32))
        m, l, acc = lax.fori_loop(0, i, past_block, init)

        kb = k_ref[0, pl.ds(own_start, MOBA_BLOCK), kv_cols].astype(BF16)
        vtb = vt_ref[0, i, kv_cols, :].astype(BF16)
        t_ = jnp.where(rel >= 0.0, _dot_nt(kb, qs) - bias, NEG)
        m_new = jnp.maximum(m, jnp.max(t_, axis=0, keepdims=True))
        p = jnp.exp(t_ - m_new)
        alpha = jnp.exp(m - m_new)
        l = alpha * l + jnp.sum(p, axis=0, keepdims=True)
        out = (alpha * acc + _dot(vtb, p.astype(BF16))) / l
        for e in range(hpg):
            h = hpg * g + e
            o_ref[0, h * HEAD_DIM:(h + 1) * HEAD_DIM, :] = out[:, e * Q_TILE:(e + 1) * Q_TILE]


def _moba_prompt(qm, km, vm, slopes):
    b, t, _ = qm.shape
    assert t % MOBA_BLOCK == 0
    nb = t // MOBA_BLOCK
    assert nb >= MOBA_TOPK
    nbp = -(-nb // SUBLANES) * SUBLANES
    vt = jnp.transpose(vm.reshape(b, nb, MOBA_BLOCK, MOBA_KV_WIDTH), (0, 1, 3, 2))
    return pl.pallas_call(
        functools.partial(_moba_prompt_kernel, nb=nb),
        grid=(b, nb),
        in_specs=[pl.BlockSpec((1, Q_TILE, MOBA_WIDTH), lambda bi, i: (bi, i, 0)),
                  pl.BlockSpec((1, t, MOBA_KV_WIDTH), lambda bi, i: (bi, 0, 0)),
                  pl.BlockSpec((1, nb, MOBA_KV_WIDTH, MOBA_BLOCK), lambda bi, i: (bi, 0, 0, 0)),
                  pl.BlockSpec(memory_space=pltpu.SMEM)],
        out_specs=pl.BlockSpec((1, MOBA_WIDTH, Q_TILE), lambda bi, i: (bi, 0, i)),
        out_shape=jax.ShapeDtypeStruct((b, MOBA_WIDTH, t), F32),
        scratch_shapes=[pltpu.VMEM((nbp, MOBA_KV_WIDTH), F32)],
        compiler_params=pltpu.CompilerParams(dimension_semantics=("parallel", "arbitrary"),
                                             vmem_limit_bytes=VMEM_LIMIT_BYTES),
        name="moba_prompt",
    )(qm, km, vt, slopes)


def _mla_key_tile(c_bf, krT, wukT, kgr, cos, sin):
    kT = _dot_nt(wukT, c_bf)
    rot, inv = _mla_key_norm_rope(kT, krT, kgr, cos, sin)
    return kT, rot, inv


def _mla_key_norm_rope(kT, krT, kgr, cos, sin):
    n_keys = kT.shape[1]
    kn = kT[:MLA_HEADS * MLA_NOPE]
    ssq = jnp.sum((kn * kn).reshape(MLA_HEADS, MLA_NOPE, n_keys), axis=1)
    krsq = jnp.sum(krT * krT, axis=0, keepdims=True)
    inv = lax.rsqrt((ssq + krsq) / MLA_QK + EPS) * (MLA_QK ** -0.5)
    x = krT * kgr
    x1, x2 = x[:MLA_HALF], x[MLA_HALF:]
    rot = jnp.concatenate([x1 * cos - x2 * sin, x1 * sin + x2 * cos], axis=0)
    return rot, inv


def _mla_prompt_kernel(q_ref, c_ref, krT_ref, wkvT_ref, kgr_ref, cos_ref, sin_ref,
                       o_ref, khat_ref, vhat_ref, *, nb):
    i = pl.program_id(1)
    n_hd = MLA_HEADS * MLA_NOPE

    @pl.when(i == 0)
    def _():
        for n in range(nb):
            rows = slice(n * KEY_TILE, (n + 1) * KEY_TILE)
            cb = c_ref[0, rows, :].astype(BF16)
            kvT, rot, inv = _mla_key_tile(cb, krT_ref[0, :, rows], wkvT_ref[...], kgr_ref[...],
                                          cos_ref[:, rows], sin_ref[:, rows])
            pad = jnp.zeros((LANES - MLA_QK, KEY_TILE), F32)
            for h in range(MLA_HEADS):
                sc = inv[h:h + 1]
                kh = jnp.concatenate([kvT[h * MLA_NOPE:(h + 1) * MLA_NOPE] * sc, rot * sc, pad], axis=0)
                khat_ref[h, n] = kh.T.astype(BF16)
                vhat_ref[h, n] = kvT[n_hd + h * MLA_V:n_hd + (h + 1) * MLA_V].astype(BF16)

    krow = lax.broadcasted_iota(jnp.int32, (KEY_TILE, Q_TILE), 0)
    qcol = lax.broadcasted_iota(jnp.int32, (KEY_TILE, Q_TILE), 1)
    causal = krow <= qcol
    heads_per_iter = 2

    def step(h, n, qb, carry, mask):
        m, l, acc = carry
        s = _dot_nt(khat_ref[h, n], qb)
        if mask is not None:
            s = jnp.where(mask, s, NEG)
        m_new = jnp.maximum(m, jnp.max(s, axis=0, keepdims=True))
        alpha = jnp.exp(m - m_new)
        p = jnp.exp(s - m_new)
        l = alpha * l + jnp.sum(p, axis=0, keepdims=True)
        acc = alpha * acc + _dot(vhat_ref[h, n], p.astype(BF16))
        return m_new, l, acc

    for h0 in range(0, MLA_HEADS, heads_per_iter):
        heads = range(h0, h0 + heads_per_iter)
        qbs = [q_ref[0, :, h * LANES:(h + 1) * LANES].astype(BF16) for h in heads]

        def past_block(n, carries):
            return tuple(step(h, n, qb, c, None) for h, qb, c in zip(heads, qbs, carries))

        init = (jnp.full((1, Q_TILE), NEG, F32), jnp.zeros((1, Q_TILE), F32), jnp.zeros((MLA_V, Q_TILE), F32))
        carries = lax.fori_loop(0, i, past_block, (init,) * heads_per_iter)
        for h, qb, c in zip(heads, qbs, carries):
            _, l, acc = step(h, i, qb, c, causal)
            o_ref[0, h * MLA_V:(h + 1) * MLA_V, :] = acc / l


def _mla_prompt(qmla, ckv, krT, w, cos_t, sin_t):
    b, t, _ = qmla.shape
    assert t % KEY_TILE == 0
    nb = t // KEY_TILE
    consts = [w['w_kvT'], w['kgr'], cos_t, sin_t]
    return pl.pallas_call(
        functools.partial(_mla_prompt_kernel, nb=nb),
        grid=(b, nb),
        in_specs=[pl.BlockSpec((1, Q_TILE, MLA_HEADS * LANES), lambda bi, i: (bi, i, 0)),
                  pl.BlockSpec((1, t, MLA_KV_LORA), lambda bi, i: (bi, 0, 0)),
                  pl.BlockSpec((1, MLA_ROPE, t), lambda bi, i: (bi, 0, 0))]
                 + [_const_spec(c.shape) for c in consts],
        out_specs=pl.BlockSpec((1, MLA_WIDTH, Q_TILE), lambda bi, i: (bi, 0, i)),
        out_shape=jax.ShapeDtypeStruct((b, MLA_WIDTH, t), F32),
        scratch_shapes=[pltpu.VMEM((MLA_HEADS, nb, KEY_TILE, LANES), BF16),
                        pltpu.VMEM((MLA_HEADS, nb, MLA_V, KEY_TILE), BF16)],
        compiler_params=pltpu.CompilerParams(dimension_semantics=("parallel", "arbitrary"),
                                             vmem_limit_bytes=VMEM_LIMIT_BYTES),
        name="mla_prompt",
    )(qmla, ckv, krT, *consts)


class _PagedFetch:
    def __init__(self, pt_ref, pools, bufs, sem, pps):
        self.pt_ref, self.pools, self.bufs, self.sem, self.pps = pt_ref, pools, bufs, sem, pps
        self.t = pl.program_id(0) * pl.num_programs(1) + pl.program_id(1)
        self.last = pl.num_programs(0) * pl.num_programs(1) - 1
        self.slot = self.t % 2

    def _copies(self, step, slot, lookup):
        out = []
        for kk in range(self.pps):
            page = self.pt_ref[step * self.pps + kk] if lookup else 0
            for which, (pool, buf) in enumerate(zip(self.pools, self.bufs)):
                out.append(pltpu.make_async_copy(pool.at[page], buf.at[slot, kk], self.sem.at[slot, which]))
        return out

    def prime(self):
        @pl.when(self.t == 0)
        def _():
            for c in self._copies(0, 0, True):
                c.start()

    def prefetch_and_wait(self):
        for c in self._copies(jnp.minimum(self.t + 1, self.last), 1 - self.slot, True):
            c.start()
        for c in self._copies(self.t, self.slot, False):
            c.wait()

    def drain(self):
        @pl.when(self.t == self.last)
        def _():
            for c in self._copies(self.t, 1 - self.slot, False):
                c.wait()


def _moba_decode_body(slot, qbd_ref, knew_ref, vnew_ref, slope_ref, tq_ref, o_ref,
                      gate_s, m_s, l_s, o_s, kbuf, vbuf, *, pps, past_len):
    j = pl.program_id(1)
    nblk = gate_s.shape[0]
    nq = qbd_ref.shape[1]
    scale = HEAD_DIM ** -0.5
    bps = pps // 2

    qbd = qbd_ref[0]
    q_hi, q_lo = _split_bf16(qbd)
    qb2 = jnp.concatenate([q_hi, q_lo], axis=0)
    qb = q_hi
    slope = slope_ref[...]
    tq = tq_ref[...]
    slope2 = jnp.concatenate([slope, slope], axis=1)
    tq2 = jnp.concatenate([tq, tq], axis=1)
    lane2 = lax.broadcasted_iota(jnp.int32, (nq, MOBA_BLOCK), 1).astype(F32)

    def page_t(buf, kk):
        return buf[slot, kk].reshape(MOBA_KV_WIDTH, buf.shape[-1]).astype(BF16)

    for bb in range(bps):
        ktb = jnp.concatenate([page_t(kbuf, 2 * bb), page_t(kbuf, 2 * bb + 1)], axis=1)
        vtb = jnp.concatenate([page_t(vbuf, 2 * bb), page_t(vbuf, 2 * bb + 1)], axis=1)
        blk = j * bps + bb
        dist = (past_len + tq2) - ((blk * MOBA_BLOCK).astype(F32) + lane2)
        qk2 = _dot(qb2, ktb)
        qk = qk2[:nq]
        gate = jnp.sum(qk + qk2[nq:], axis=-1, keepdims=True) / MOBA_BLOCK
        s = qk * scale - slope2 * dist
        m = jnp.max(s, axis=-1, keepdims=True)
        p = jnp.exp(s - m)
        l = jnp.sum(p, axis=-1, keepdims=True)
        gate_s[blk] = jnp.broadcast_to(gate, (nq, LANES))
        m_s[blk] = jnp.broadcast_to(m, (nq, LANES))
        l_s[blk] = jnp.broadcast_to(l, (nq, LANES))
        o_s[blk] = _dot_nt(p.astype(BF16), vtb)

    @pl.when(j == pl.num_programs(1) - 1)
    def _():
        gates = gate_s[...]
        bidx = lax.broadcasted_iota(jnp.int32, gates.shape, 0)
        picked = jnp.zeros(gates.shape, F32)
        for _ in range(min(MOBA_TOPK, nblk)):
            mx = jnp.max(gates, axis=0, keepdims=True)
            first = jnp.min(jnp.where(gates == mx, bidx, nblk), axis=0, keepdims=True)
            hit = bidx == first
            picked = jnp.where(hit, 1.0, picked)
            gates = jnp.where(hit, -jnp.inf, gates)
        sel = picked > 0.5

        n_new = knew_ref.shape[1]
        tk = lax.broadcasted_iota(jnp.int32, (nq, n_new), 1).astype(F32)
        tq_n = tq[:, :n_new]
        dist_o = tq_n - tk
        s_o = _dot_nt(qb, knew_ref[0].astype(BF16)) * scale - slope[:, :n_new] * dist_o
        mask_o = dist_o >= 0.0
        s_o = jnp.where(mask_o, s_o, NEG)
        m_o = jnp.broadcast_to(jnp.max(s_o, axis=-1, keepdims=True), (nq, LANES))

        m_all = m_s[...]
        m_fin = jnp.maximum(jnp.max(jnp.where(sel, m_all, NEG), axis=0), m_o)
        wgt = jnp.exp(jnp.where(sel, m_all - m_fin[None], -jnp.inf))
        p_o = jnp.where(mask_o, jnp.exp(s_o - m_fin[:, :n_new]), 0.0)
        l_fin = jnp.sum(wgt * l_s[...], axis=0) + jnp.sum(p_o, axis=-1, keepdims=True)
        wgt2 = jnp.concatenate([wgt, wgt], axis=2)
        o_fin = jnp.sum(wgt2 * o_s[...], axis=0) + _dot(p_o.astype(BF16), vnew_ref[0].astype(BF16))
        o_fin = o_fin / jnp.concatenate([l_fin, l_fin], axis=1)

        row = lax.broadcasted_iota(jnp.int32, (nq, HEAD_DIM), 0)
        grp = (row % MOBA_HEADS) // (MOBA_HEADS // MOBA_KV_HEADS)
        out = jnp.zeros((nq, HEAD_DIM), F32)
        for g in range(MOBA_KV_HEADS):
            out = jnp.where(grp == g, o_fin[:, g * HEAD_DIM:(g + 1) * HEAD_DIM], out)
        o_ref[0] = out


def _mla_decode_body(buf_slot, qbdn_ref, qr_ref, cnew_ref, krnew_ref, tq_ref, wukT_ref, wuv_ref, kgr_ref,
                     cos_ref, sin_ref, cosn_ref, sinn_ref, o_ref, lhs_s, m_s, l_s, acc_s, cbuf, rbuf,
                     *, pps, sub, n_tok):
    j = pl.program_id(1)
    nq = qr_ref.shape[1]
    page = cbuf.shape[2]
    n_hd = MLA_HEADS * MLA_NOPE

    @pl.when(j == 0)
    def _():
        lhs_s[:n_hd, :] = wukT_ref[...]
        lhs_s[n_hd:, :] = _dot(qbdn_ref[0].astype(BF16), wukT_ref[...]).astype(BF16)

    qr = qr_ref[0].astype(BF16)

    def partial_softmax(cb, krT, cos, sin, mask, kT=None):
        if kT is None:
            kT = _dot_nt(lhs_s[...], cb)
        rot, inv = _mla_key_norm_rope(kT, krT, kgr_ref[...], cos, sin)
        inv_rows = jnp.concatenate([inv] * (nq // MLA_HEADS), axis=0)
        s = (kT[n_hd:] + _dot(qr, rot.astype(BF16))) * inv_rows
        if mask is not None:
            s = jnp.where(mask, s, NEG)
        m = jnp.max(s, axis=-1, keepdims=True)
        p = jnp.exp(s - m)
        if mask is not None:
            p = jnp.where(mask, p, 0.0)
        return m, jnp.sum(p, axis=-1, keepdims=True), _dot(p.astype(BF16), cb)

    n_sub = pps // sub
    latent = lambda sc: jnp.concatenate([cbuf[buf_slot, k] for k in range(sc * sub, (sc + 1) * sub)],
                                        axis=0).astype(BF16)
    cb_next = latent(0)
    kT_next = _dot_nt(lhs_s[...], cb_next)
    for sc in range(n_sub):
        cb, kT = cb_next, kT_next
        if sc + 1 < n_sub:
            cb_next = latent(sc + 1)
            kT_next = _dot_nt(lhs_s[...], cb_next)
        krT = jnp.concatenate([rbuf[buf_slot, k] for k in range(sc * sub, (sc + 1) * sub)], axis=1)
        cols = slice(sc * sub * page, (sc + 1) * sub * page)
        m, l, acc = partial_softmax(cb, krT, cos_ref[:, cols], sin_ref[:, cols], None, kT)
        part = j * (pps // sub) + sc
        m_s[part] = jnp.broadcast_to(m, (nq, LANES))
        l_s[part] = jnp.broadcast_to(l, (nq, LANES))
        acc_s[part] = acc

    @pl.when(j == pl.num_programs(1) - 1)
    def _():
        tk = lax.broadcasted_iota(jnp.int32, (nq, LANES), 1).astype(F32)
        mask = (tk <= tq_ref[...]) & (tk < n_tok)
        m_n, l_n, acc_n = partial_softmax(cnew_ref[0].astype(BF16), krnew_ref[0], cosn_ref[...], sinn_ref[...], mask)
        m_all = m_s[...]
        m_fin = jnp.maximum(jnp.max(m_all, axis=0), m_n)
        wgt = jnp.exp(m_all - m_fin[None])
        w_n = jnp.exp(m_n - m_fin)
        l_fin = jnp.sum(wgt * l_s[...], axis=0) + w_n * l_n
        a_fin = jnp.sum(wgt * acc_s[...], axis=0) + w_n * acc_n
        o_lat = (a_fin / l_fin).astype(BF16)
        full = _dot(o_lat, wuv_ref[...])
        row = lax.broadcasted_iota(jnp.int32, (nq, MLA_V), 0)
        head = row % MLA_HEADS
        out = jnp.zeros((nq, MLA_V), F32)
        for h in range(MLA_HEADS):
            out = jnp.where(head == h, full[:, h * MLA_V:(h + 1) * MLA_V], out)
        o_ref[0] = out


def _decode_kernel(pt_ref, qbd_ref, knew_ref, vnew_ref, slope_ref, tq_ref,
                   qbdn_ref, qr_ref, cnew_ref, krnew_ref, wukT_ref, wuv_ref, kgr_ref, cos_ref, sin_ref, cosn_ref, sinn_ref,
                   kt_hbm, vt_hbm, c_hbm, rt_hbm, o_moba_ref, o_mla_ref,
                   gate_s, mm_s, lm_s, om_s, lhs_s, ml_s, ll_s, accl_s, kbuf, vbuf, cbuf, rbuf, sem,
                   *, pps, past_len, sub, n_tok):
    fetch = _PagedFetch(pt_ref, (kt_hbm, vt_hbm, c_hbm, rt_hbm), (kbuf, vbuf, cbuf, rbuf), sem, pps)
    fetch.prime()
    fetch.prefetch_and_wait()
    _moba_decode_body(fetch.slot, qbd_ref, knew_ref, vnew_ref, slope_ref, tq_ref, o_moba_ref,
                      gate_s, mm_s, lm_s, om_s, kbuf, vbuf, pps=pps, past_len=past_len)
    _mla_decode_body(fetch.slot, qbdn_ref, qr_ref, cnew_ref, krnew_ref, tq_ref, wukT_ref, wuv_ref, kgr_ref,
                     cos_ref, sin_ref, cosn_ref, sinn_ref, o_mla_ref, lhs_s, ml_s, ll_s, accl_s, cbuf, rbuf,
                     pps=pps, sub=sub, n_tok=n_tok)
    fetch.drain()


def _decode(page_flat, qbd, knew, vnew, slope_rows, tq_rows, qbdn, qr, cnew, krnew_t, w, cos_p, sin_p, cos_n, sin_n,
            pool_kt, pool_vt, pool_c, pool_rt, n_pages, page, n_tok):
    s, nq, _ = qbd.shape
    assert 2 * page == MOBA_BLOCK and page == LANES and pool_kt.shape[1:] == (MOBA_KV_HEADS, HEAD_DIM, page)
    pps = min(DECODE_PAGES, n_pages)
    sub = min(MLA_DECODE_SUB_PAGES, pps)
    assert n_pages % pps == 0 and pps % 2 == 0 and pps % sub == 0
    nblk = n_pages // 2
    n_slots = n_pages // sub
    n_new = knew.shape[1]
    n_lhs = MLA_HEADS * MLA_NOPE + nq

    per_seq = lambda shape: pl.BlockSpec((1,) + shape, lambda si, j, pt: (si, 0, 0))
    cst = lambda a: pl.BlockSpec(a.shape, lambda si, j, pt: (0,) * a.ndim)
    tab = pl.BlockSpec((MLA_HALF, pps * page), lambda si, j, pt: (0, j))
    hbm = pl.BlockSpec(memory_space=pl.ANY)
    kv_buf = pltpu.VMEM((2, pps, MOBA_KV_HEADS, HEAD_DIM, page), F32)
    grid_spec = pltpu.PrefetchScalarGridSpec(
        num_scalar_prefetch=1,
        grid=(s, n_pages // pps),
        in_specs=[per_seq((nq, MOBA_KV_WIDTH)), per_seq((n_new, MOBA_KV_WIDTH)), per_seq((n_new, MOBA_KV_WIDTH)),
                  cst(slope_rows), cst(tq_rows),
                  per_seq((nq, MLA_HEADS * MLA_NOPE)), per_seq((nq, MLA_ROPE)),
                  per_seq((LANES, MLA_KV_LORA)), per_seq((MLA_ROPE, LANES)),
                  cst(w['w_ukT']), cst(w['w_uv']), cst(w['kgr']), tab, tab, cst(cos_n), cst(sin_n),
                  hbm, hbm, hbm, hbm],
        out_specs=[per_seq((nq, HEAD_DIM)), per_seq((nq, MLA_V))],
        scratch_shapes=[pltpu.VMEM((nblk, nq, LANES), F32)] * 3 + [pltpu.VMEM((nblk, nq, MOBA_KV_WIDTH), F32)]
                       + [pltpu.VMEM((n_lhs, MLA_KV_LORA), BF16), pltpu.VMEM((n_slots, nq, LANES), F32),
                          pltpu.VMEM((n_slots, nq, LANES), F32), pltpu.VMEM((n_slots, nq, MLA_KV_LORA), F32)]
                       + [kv_buf, kv_buf, pltpu.VMEM((2, pps, page, MLA_KV_LORA), F32),
                          pltpu.VMEM((2, pps, MLA_ROPE, page), F32), pltpu.SemaphoreType.DMA((2, 4))],
    )
    return pl.pallas_call(
        functools.partial(_decode_kernel, pps=pps, past_len=float(n_pages * page), sub=sub, n_tok=float(n_tok)),
        grid_spec=grid_spec,
        out_shape=[jax.ShapeDtypeStruct((s, nq, HEAD_DIM), F32), jax.ShapeDtypeStruct((s, nq, MLA_V), F32)],
        compiler_params=pltpu.CompilerParams(dimension_semantics=("arbitrary", "arbitrary"),
                                             vmem_limit_bytes=VMEM_LIMIT_BYTES),
        name="decode",
    )(page_flat, qbd, knew, vnew, slope_rows, tq_rows, qbdn, qr, cnew, krnew_t, w['w_ukT'], w['w_uv'], w['kgr'],
      cos_p, sin_p, cos_n, sin_n, pool_kt, pool_vt, pool_c, pool_rt)


def _out_mlp_kernel(x_ref, om_ref, ol_ref, gom_ref, gol_ref, wo_ref, ln2_ref, wup_ref, wdn_ref, y_ref):
    mixed = jnp.concatenate([_rms(om_ref[...], gom_ref[...]), _rms(ol_ref[...], gol_ref[...])], axis=-1)
    x2 = x_ref[...] + _dot(mixed.astype(BF16), wo_ref[...])
    hb = _rms(x2, ln2_ref[...]).astype(BF16)
    acc = x2
    for c in range(D_FF // D_MODEL):
        cols = slice(c * D_MODEL, (c + 1) * D_MODEL)
        u = jnp.maximum(_dot(hb, wup_ref[:, cols]), 0.0)
        acc = acc + _dot((u * u).astype(BF16), wdn_ref[cols, :])
    y_ref[...] = acc


def _out_mlp(x2d, om, ol, w):
    rows = x2d.shape[0]
    tm = min(ROW_TILE, rows)
    assert rows % tm == 0
    row_spec = lambda width: pl.BlockSpec((tm, width), lambda i: (i, 0))
    single = lambda a: pl.BlockSpec(a.shape, lambda i: (0,) * a.ndim, pipeline_mode=pl.Buffered(1))
    consts = [w['gom'], w['gol'], w['w_o'], w['ln2'], w['w_up'], w['w_down']]
    return pl.pallas_call(
        _out_mlp_kernel,
        grid=(rows // tm,),
        in_specs=[row_spec(D_MODEL), row_spec(MOBA_WIDTH), row_spec(MLA_WIDTH)] + [single(c) for c in consts],
        out_specs=row_spec(D_MODEL),
        out_shape=jax.ShapeDtypeStruct((rows, D_MODEL), F32),
        compiler_params=pltpu.CompilerParams(dimension_semantics=("parallel",), vmem_limit_bytes=VMEM_LIMIT_BYTES),
        name="out_mlp",
    )(x2d, om, ol, *consts)


def _rope_angles(pos):
    inv_freq = ROPE_BASE ** (-jnp.arange(MLA_HALF, dtype=F32) / MLA_HALF)
    ang = pos.astype(F32)[:, None] * inv_freq[None, :]
    return jnp.cos(ang), jnp.sin(ang)


def _query_rope_tables(pos):
    cos, sin = _rope_angles(pos)
    n = pos.shape[0]
    rc = jnp.concatenate([jnp.ones((n, MLA_NOPE), F32), cos, cos, jnp.zeros((n, LANES - MLA_QK), F32)], axis=1)
    zeros = lambda k: jnp.zeros((n, k), F32)
    rs1 = jnp.concatenate([zeros(MLA_NOPE + MLA_HALF), sin, zeros(LANES - MLA_QK)], axis=1)
    rs2 = jnp.concatenate([zeros(MLA_NOPE), -sin, zeros(LANES - MLA_NOPE - MLA_HALF)], axis=1)
    return rc, rs1, rs2


def _head_block(v):
    return jnp.concatenate([v, jnp.zeros((LANES - MLA_QK,), F32)])[None, :]


def _prep_weights(ln1_g, w_in, moba_q_g, moba_k_g, mla_q_lora_g, w_uq, mla_q_g, mla_kv_lora_g, w_uk, w_uv,
                  mla_k_g, out_g_moba, out_g_mla, w_o, ln2_g, w_up, w_down):
    w_uq_h = w_uq.reshape(MLA_Q_LORA, MLA_HEADS, MLA_QK)
    w_uq_h = jnp.pad(w_uq_h, ((0, 0), (0, 0), (0, LANES - MLA_QK))).reshape(MLA_Q_LORA, MLA_HEADS * LANES)
    kgn = jnp.concatenate([mla_k_g[:MLA_NOPE], jnp.ones((LANES - MLA_NOPE,), F32)])[None, :]
    return {
        'ln1': ln1_g[None, :],
        'w_main': w_in[:, :MAIN_WIDTH].astype(BF16),
        'w_kr': jnp.pad(w_in[:, MAIN_WIDTH:], ((0, 0), (0, LANES - MLA_ROPE))).astype(BF16),
        'gq': jnp.tile(moba_q_g, 2)[None, :],
        'gk': jnp.tile(moba_k_g, 2)[None, :],
        'gql': mla_q_lora_g[None, :],
        'w_uq': w_uq_h.astype(BF16),
        'gmq': _head_block(mla_q_g),
        'kgn': kgn,
        'gkv': mla_kv_lora_g[None, :],
        'w_ukT': w_uk.reshape(MLA_KV_LORA, MLA_HEADS * MLA_NOPE).T.astype(BF16),
        'w_uv': w_uv.reshape(MLA_KV_LORA, MLA_HEADS * MLA_V).astype(BF16),
        'w_kvT': jnp.concatenate([w_uk.reshape(MLA_KV_LORA, MLA_HEADS * MLA_NOPE).T,
                                  w_uv.reshape(MLA_KV_LORA, MLA_HEADS * MLA_V).T]).astype(BF16),
        'kgr': mla_k_g[MLA_NOPE:][:, None],
        'gom': out_g_moba[None, :],
        'gol': out_g_mla[None, :],
        'w_o': w_o.astype(BF16),
        'ln2': ln2_g[None, :],
        'w_up': w_up.astype(BF16),
        'w_down': w_down.astype(BF16),
    }


def _alibi_slopes():
    return jnp.exp2(-8.0 * jnp.arange(1, MOBA_HEADS + 1, dtype=F32) / MOBA_HEADS)


def _prompt_layer(x, w):
    b, t, _ = x.shape
    pos = jnp.arange(t, dtype=jnp.int32)
    x2d = x.reshape(b * t, D_MODEL)
    tm = min(ROW_TILE, b * t)
    assert t % tm == 0
    qm, km, vm, qmla, ckv, kr = _inproj(x2d, _query_rope_tables(pos), t // tm, w)
    o_moba_t = _moba_prompt(qm.reshape(b, t, -1), km.reshape(b, t, -1), vm.reshape(b, t, -1), _alibi_slopes())
    o_moba = jnp.transpose(o_moba_t, (0, 2, 1))
    cos, sin = _rope_angles(pos)
    krT = jnp.transpose(kr.reshape(b, t, -1), (0, 2, 1))
    o_mla_t = _mla_prompt(qmla.reshape(b, t, -1), ckv.reshape(b, t, -1), krT, w, cos.T, sin.T)
    o_mla = jnp.transpose(o_mla_t, (0, 2, 1))
    y = _out_mlp(x2d, o_moba.reshape(b * t, -1), o_mla.reshape(b * t, -1), w)
    return (y.reshape(b, t, D_MODEL), km.reshape(b, t, MOBA_KV_HEADS, HEAD_DIM),
            vm.reshape(b, t, MOBA_KV_HEADS, HEAD_DIM), ckv.reshape(b, t, MLA_KV_LORA), kr.reshape(b, t, MLA_ROPE))


def _pad_rows(a, n):
    return jnp.pad(a, ((0, 0), (0, n - a.shape[1]), (0, 0)))


def _sample_layer(x, w, pool_k, pool_v, pool_c, pool_r, page_table):
    s, t, _ = x.shape
    n_pages = page_table.shape[1]
    page = pool_k.shape[1]
    past_len = n_pages * page
    nq = t * MOBA_HEADS
    x2d = x.reshape(s * t, D_MODEL)
    tm = min(ROW_TILE, s * t)
    assert tm % t == 0
    pos_new = past_len + jnp.arange(t, dtype=jnp.int32)
    qm, km, vm, qmla, ckv, kr = _inproj(x2d, _query_rope_tables(jnp.tile(pos_new, tm // t)), 1, w)

    page_flat = page_table.reshape(-1)
    row = jnp.arange(nq)
    tq_rows = jnp.broadcast_to((row // MOBA_HEADS).astype(F32)[:, None], (nq, LANES))
    slope_rows = jnp.broadcast_to(_alibi_slopes()[row % MOBA_HEADS][:, None], (nq, LANES))

    kv_onehot = jax.nn.one_hot(jnp.arange(MOBA_HEADS) // (MOBA_HEADS // MOBA_KV_HEADS), MOBA_KV_HEADS, dtype=F32)
    qbd = (qm.reshape(s, t, MOBA_HEADS, 1, HEAD_DIM) * kv_onehot[None, None, :, :, None]).reshape(s, nq, MOBA_KV_WIDTH)
    n_new = -(-t // (2 * SUBLANES)) * (2 * SUBLANES)
    page_t = lambda pool: jnp.transpose(pool, (0, 2, 3, 1))

    qh = qmla.reshape(s, t, MLA_HEADS, LANES)
    head_eye = jnp.eye(MLA_HEADS, dtype=F32)
    qbdn = (qh[..., None, :MLA_NOPE] * head_eye[None, None, :, :, None]).reshape(s, nq, MLA_HEADS * MLA_NOPE)
    qr = qh[..., MLA_NOPE:MLA_QK].reshape(s, nq, MLA_ROPE)
    cos_p, sin_p = _rope_angles(jnp.arange(past_len, dtype=jnp.int32))
    cos_n, sin_n = _rope_angles(past_len + jnp.arange(LANES, dtype=jnp.int32))
    krnew_t = jnp.transpose(_pad_rows(kr.reshape(s, t, -1), LANES), (0, 2, 1))
    o_moba, o_mla = _decode(page_flat, qbd, _pad_rows(km.reshape(s, t, -1), n_new), _pad_rows(vm.reshape(s, t, -1), n_new),
                            slope_rows, tq_rows, qbdn, qr, _pad_rows(ckv.reshape(s, t, -1), LANES), krnew_t, w,
                            cos_p.T, sin_p.T, cos_n.T, sin_n.T, page_t(pool_k), page_t(pool_v), pool_c,
                            jnp.transpose(pool_r, (0, 2, 1)), n_pages, page, t)

    y = _out_mlp(x2d, o_moba.reshape(s * t, MOBA_WIDTH), o_mla.reshape(s * t, MLA_WIDTH), w)
    return (y.reshape(s, t, D_MODEL), km.reshape(s, t, MOBA_KV_HEADS, HEAD_DIM),
            vm.reshape(s, t, MOBA_KV_HEADS, HEAD_DIM), ckv.reshape(s, t, MLA_KV_LORA), kr.reshape(s, t, MLA_ROPE))


def kernel(x_prompt, x_sample, cache_moba_k, cache_moba_v, cache_mla_ckv, cache_mla_krope, page_table, ln1_g, w_in,
           moba_q_g, moba_k_g, mla_q_lora_g, w_uq, mla_q_g, mla_kv_lora_g, w_uk, w_uv, mla_k_g, out_g_moba,
           out_g_mla, w_o, ln2_g, w_up, w_down):
    depth = w_in.shape[0]
    layer_weights = (ln1_g, w_in, moba_q_g, moba_k_g, mla_q_lora_g, w_uq, mla_q_g, mla_kv_lora_g, w_uk, w_uv,
                     mla_k_g, out_g_moba, out_g_mla, w_o, ln2_g, w_up, w_down)
    y_p, y_s = x_prompt, x_sample
    outs_p, outs_s = [], []
    for layer in range(depth):
        w = _prep_weights(*(a[layer] for a in layer_weights))
        y_p, *new_p = _prompt_layer(y_p, w)
        y_s, *new_s = _sample_layer(y_s, w, cache_moba_k[layer], cache_moba_v[layer], cache_mla_ckv[layer],
                                    cache_mla_krope[layer], page_table)
        outs_p.append(new_p)
        outs_s.append(new_s)
    stack = lambda outs, k: jnp.stack([o[k] for o in outs])
    return (y_p, y_s, stack(outs_p, 0), stack(outs_p, 1), stack(outs_p, 2), stack(outs_p, 3),
            stack(outs_s, 0), stack(outs_s, 1), stack(outs_s, 2), stack(outs_s, 3))
```
